```python
import math
import jax, jax.numpy as jnp
from jax import lax
import numpy as np

D_MODEL = 1024
BATCH = 32
SEQ = 2048
DEPTH = 4

HEAD_DIM = 64
QBLOCK = 128
A_HEADS = 8
A_KV = 2
A_WINDOW = 128
B_HEADS = 8
B_KV = 2
CMP_BLOCK = 32
CMP_STRIDE = 16
CMP_HIDDEN = 128
SEL_BLOCK = 64
SEL_TOPN = 16
B_WINDOW = 512
SEL_QCHUNK = 16
C_HEADS = 8
BRANCH_WIDTH = 512
N_BRANCH = 3
REL_BUCKETS = 32
REL_MAX_DIST = 128
D_FF = 2816
FFN_RES = 0.5
N_SUB = 3
EPS = 1e-6
IN_WIDTH = (A_HEADS + 2 * A_KV) * HEAD_DIM + (B_HEADS + 6 * B_KV) * HEAD_DIM + 3 * B_HEADS + 3 * C_HEADS * HEAD_DIM + N_BRANCH * D_MODEL

kernel_name = "hybrid_gated_swa_nsa_stickbreak_macaron"


def in_layout():
    widths = [
        ("a_q", A_HEADS * HEAD_DIM), ("a_k", A_KV * HEAD_DIM), ("a_v", A_KV * HEAD_DIM),
        ("b_q", B_HEADS * HEAD_DIM),
        ("b_k_cmp", B_KV * HEAD_DIM), ("b_v_cmp", B_KV * HEAD_DIM),
        ("b_k_slc", B_KV * HEAD_DIM), ("b_v_slc", B_KV * HEAD_DIM),
        ("b_k_win", B_KV * HEAD_DIM), ("b_v_win", B_KV * HEAD_DIM),
        ("b_gate", 3 * B_HEADS),
        ("c_q", C_HEADS * HEAD_DIM), ("c_k", C_HEADS * HEAD_DIM), ("c_v", C_HEADS * HEAD_DIM),
        ("merge_gate", N_BRANCH * D_MODEL),
    ]
    out, off = {}, 0
    for name, w in widths:
        out[name] = (off, off + w)
        off += w
    return out


def rms_norm(x, gain):
    x32 = x.astype(jnp.float32)
    y = x32 * lax.rsqrt(jnp.mean(x32 * x32, axis=-1, keepdims=True) + EPS)
    return (y * gain.astype(jnp.float32)).astype(x.dtype)


def swiglu(h, w_gate, w_up, w_down):
    return (jax.nn.silu(h @ w_gate) * (h @ w_up)) @ w_down


def t5_bucket(dist):
    max_exact = REL_BUCKETS // 2
    d = jnp.maximum(dist, 0)
    ratio = jnp.log(jnp.maximum(d, 1).astype(jnp.float32) / max_exact) / math.log(REL_MAX_DIST / max_exact)
    large = jnp.minimum(max_exact + (ratio * (REL_BUCKETS - max_exact)).astype(jnp.int32), REL_BUCKETS - 1)
    return jnp.where(d < max_exact, d, large)


def masked_softmax(logits, mask):
    logits = jnp.where(mask, logits.astype(jnp.float32), -1e30)
    m = jnp.max(logits, axis=-1, keepdims=True)
    e = jnp.where(mask, jnp.exp(logits - m), 0.0)
    s = jnp.sum(e, axis=-1, keepdims=True)
    return e / jnp.where(s > 0, s, 1.0)


def band_bias(table, window):
    dist = window + jnp.arange(QBLOCK)[:, None] - jnp.arange(window + QBLOCK)[None, :]
    return jnp.moveaxis(table[t5_bucket(dist)], -1, 0).astype(jnp.float32)


def banded_attention(q, k, v, bias, window, sink):
    bsz, g, hpg, s, dh = q.shape
    nblk = s // QBLOCK
    span = window + QBLOCK
    kp = jnp.pad(k, ((0, 0), (0, 0), (window, 0), (0, 0)))
    vp = jnp.pad(v, ((0, 0), (0, 0), (window, 0), (0, 0)))
    qb = jnp.moveaxis(q.reshape(bsz, g, hpg, nblk, QBLOCK, dh), 3, 0)
    iq = jnp.arange(QBLOCK)[:, None]
    jk = jnp.arange(span)[None, :]
    dist = window + iq - jk
    in_band = (dist >= 0) & (dist < window)
    scale = 1.0 / math.sqrt(dh)

    def block(args):
        qi, blk = args
        start = blk * QBLOCK
        kb = lax.dynamic_slice_in_dim(kp, start, span, axis=2)
        vb = lax.dynamic_slice_in_dim(vp, start, span, axis=2)
        mask = in_band & (start - window + jk >= 0)
        logits = jnp.einsum('bghqd,bgkd->bghqk', qi, kb).astype(jnp.float32) * scale + bias
        logits = jnp.where(mask, logits, -1e30)
        if sink is None:
            p = jax.nn.softmax(logits, axis=-1)
        else:
            sk = sink.astype(jnp.float32)[None, :, :, None, None]
            m = jnp.maximum(jnp.max(logits, axis=-1, keepdims=True), sk)
            e = jnp.exp(logits - m)
            p = e / (jnp.sum(e, axis=-1, keepdims=True) + jnp.exp(sk - m))
        return jnp.einsum('bghqk,bgkd->bghqd', p.astype(vb.dtype), vb)

    out = lax.map(block, (qb, jnp.arange(nblk)))
    return jnp.moveaxis(out, 0, 3).reshape(bsz, g, hpg, s, dh)


def swa_sink_attention(q, k, v, sinks, rel_table_a):
    bsz, s, _ = q.shape
    hpg = A_HEADS // A_KV
    qh = q.reshape(bsz, s, A_KV, hpg, HEAD_DIM).transpose(0, 2, 3, 1, 4)
    kh = k.reshape(bsz, s, A_KV, HEAD_DIM).transpose(0, 2, 1, 3)
    vh = v.reshape(bsz, s, A_KV, HEAD_DIM).transpose(0, 2, 1, 3)
    bias = band_bias(rel_table_a, A_WINDOW).reshape(A_KV, hpg, QBLOCK, A_WINDOW + QBLOCK)
    o = banded_attention(qh, kh, vh, bias, A_WINDOW, sinks.reshape(A_KV, hpg))
    return o.transpose(0, 3, 1, 2, 4).reshape(bsz, s, A_HEADS * HEAD_DIM)


def nsa_compress(kv, pos, w1, w2):
    bsz, s, _ = kv.shape
    ncmp = (s - CMP_BLOCK) // CMP_STRIDE + 1
    kv_r = kv.reshape(bsz, s, B_KV, HEAD_DIM)
    idx = jnp.arange(ncmp)[:, None] * CMP_STRIDE + jnp.arange(CMP_BLOCK)[None, :]
    blocks = kv_r[:, idx] + pos[None, None, :, None, :]
    flat = blocks.transpose(0, 3, 1, 2, 4).reshape(bsz, B_KV, ncmp, CMP_BLOCK * HEAD_DIM)
    return jax.nn.gelu(flat @ w1) @ w2


def nsa_selected_attention(qh, k_slc, v_slc, sel_idx, rel_table_b):
    bsz, g, hpg, s, dh = qh.shape
    nsel = s // SEL_BLOCK
    n_top = sel_idx.shape[-1]
    nk = n_top * SEL_BLOCK
    nchunk = s // SEL_QCHUNK
    kb = k_slc.reshape(bsz, nsel, SEL_BLOCK, g, dh).transpose(0, 3, 1, 2, 4)
    vb = v_slc.reshape(bsz, nsel, SEL_BLOCK, g, dh).transpose(0, 3, 1, 2, 4)
    qc = jnp.moveaxis(qh.reshape(bsz, g, hpg, nchunk, SEL_QCHUNK, dh), 3, 0)
    ic = jnp.moveaxis(sel_idx.reshape(bsz, g, nchunk, SEL_QCHUNK, n_top), 2, 0)
    tc = jnp.arange(s).reshape(nchunk, SEL_QCHUNK)
    table = rel_table_b.reshape(REL_BUCKETS, g, hpg).transpose(1, 0, 2)
    gidx = jnp.arange(g)[None, :, None, None]
    gather = jax.vmap(jax.vmap(lambda blocks, ix: blocks[ix]))
    scale = 1.0 / math.sqrt(dh)

    def chunk(args):
        qi, ii, ti = args
        kg = gather(kb, ii).reshape(bsz, g, SEL_QCHUNK, nk, dh)
        vg = gather(vb, ii).reshape(bsz, g, SEL_QCHUNK, nk, dh)
        kpos = (ii[..., None] * SEL_BLOCK + jnp.arange(SEL_BLOCK)).reshape(bsz, g, SEL_QCHUNK, nk)
        dist = ti[None, None, :, None] - kpos
        bias = jnp.moveaxis(table[gidx, t5_bucket(dist)], -1, 2)
        logits = jnp.einsum('bghqd,bgqkd->bghqk', qi, kg).astype(jnp.float32) * scale + bias
        p = masked_softmax(logits, (dist >= 0)[:, :, None])
        return jnp.einsum('bghqk,bgqkd->bghqd', p.astype(vg.dtype), vg)

    out = lax.map(chunk, (qc, ic, tc))
    return jnp.moveaxis(out, 0, 3).reshape(bsz, g, hpg, s, dh)


def nsa_attention(q, k_cmp, v_cmp, k_slc, v_slc, k_win, v_win, gate_logits, cmp_pos, cmp_w1, cmp_w2, rel_table_b):
    bsz, s, _ = q.shape
    g, hpg = B_KV, B_HEADS // B_KV
    qh = q.reshape(bsz, s, g, hpg, HEAD_DIM).transpose(0, 2, 3, 1, 4)
    t = jnp.arange(s)
    scale = 1.0 / math.sqrt(HEAD_DIM)
    kc = nsa_compress(k_cmp, cmp_pos[0], cmp_w1[0], cmp_w2[0])
    vc = nsa_compress(v_cmp, cmp_pos[1], cmp_w1[1], cmp_w2[1])
    ncmp = kc.shape[2]
    cstart = jnp.arange(ncmp) * CMP_STRIDE
    cmp_mask = (cstart + CMP_BLOCK - 1)[None, :] <= t[:, None]
    logits_c = jnp.einsum('bghsd,bgnd->bghsn', qh, kc).astype(jnp.float32) * scale
    p_cmp = masked_softmax(logits_c, cmp_mask)
    o_cmp = jnp.einsum('bghsn,bgnd->bghsd', p_cmp.astype(vc.dtype), vc)
    nsel = s // SEL_BLOCK
    sstart = jnp.arange(nsel) * SEL_BLOCK
    overlap = ((cstart[:, None] < sstart[None, :] + SEL_BLOCK) & (cstart[:, None] + CMP_BLOCK > sstart[None, :])).astype(jnp.float32)
    imp = jnp.einsum('bgsn,nj->bgsj', jnp.sum(p_cmp, axis=2), overlap)
    cur = t // SEL_BLOCK
    jsel = jnp.arange(nsel)[None, :]
    forced = (jsel == 0) | (jsel == cur[:, None]) | (jsel == cur[:, None] - 1)
    future = sstart[None, :] > t[:, None]
    prio = jnp.where(forced, jnp.inf, jnp.where(future, -jnp.inf, imp))
    _, sel_idx = lax.top_k(prio, min(SEL_TOPN, nsel))
    o_slc = nsa_selected_attention(qh, k_slc, v_slc, sel_idx, rel_table_b)
    kw = k_win.reshape(bsz, s, g, HEAD_DIM).transpose(0, 2, 1, 3)
    vw = v_win.reshape(bsz, s, g, HEAD_DIM).transpose(0, 2, 1, 3)
    bias_w = band_bias(rel_table_b, B_WINDOW).reshape(g, hpg, QBLOCK, B_WINDOW + QBLOCK)
    o_win = banded_attention(qh, kw, vw, bias_w, B_WINDOW, None)
    gates = jax.nn.sigmoid(gate_logits.astype(jnp.float32)).reshape(bsz, s, 3, g, hpg).transpose(2, 0, 3, 4, 1)[..., None]
    gates = gates.astype(q.dtype)
    o = gates[0] * o_cmp + gates[1] * o_slc + gates[2] * o_win
    return o.transpose(0, 3, 1, 2, 4).reshape(bsz, s, B_HEADS * HEAD_DIM)


def stick_breaking_attention(q, k, v):
    bsz, s, _ = q.shape
    qh = q.reshape(bsz, s, C_HEADS, HEAD_DIM).transpose(0, 2, 1, 3)
    kh = k.reshape(bsz, s, C_HEADS, HEAD_DIM).transpose(0, 2, 1, 3)
    vh = v.reshape(bsz, s, C_HEADS, HEAD_DIM).transpose(0, 2, 1, 3)
    scale = 1.0 / math.sqrt(HEAD_DIM)
    outs = []
    for blk in range(s // QBLOCK):
        s0, s1 = blk * QBLOCK, (blk + 1) * QBLOCK
        z = jnp.einsum('bhqd,bhkd->bhqk', qh[:, :, s0:s1], kh[:, :, :s1]).astype(jnp.float32) * scale
        strict = jnp.arange(s1)[None, :] < jnp.arange(s0, s1)[:, None]
        log_keep = jnp.where(strict, jax.nn.log_sigmoid(-z), 0.0)
        later = lax.cumsum(log_keep, axis=3, reverse=True) - log_keep
        w = jnp.where(strict, jnp.exp(jax.nn.log_sigmoid(z) + later), 0.0)
        outs.append(jnp.einsum('bhqk,bhkd->bhqd', w.astype(vh.dtype), vh[:, :, :s1]))
    o = jnp.concatenate(outs, axis=2)
    return o.transpose(0, 2, 1, 3).reshape(bsz, s, C_HEADS * HEAD_DIM)


def token_mixing(u, w_in, sinks, cmp_pos, cmp_w1, cmp_w2, w_branch, w_out, rel_bias):
    lay = in_layout()

    def proj(name):
        a, b = lay[name]
        return u @ w_in[:, a:b]

    y_a = swa_sink_attention(proj("a_q"), proj("a_k"), proj("a_v"), sinks, rel_bias[:, :A_HEADS])
    y_b = nsa_attention(proj("b_q"), proj("b_k_cmp"), proj("b_v_cmp"), proj("b_k_slc"), proj("b_v_slc"),
                        proj("b_k_win"), proj("b_v_win"), proj("b_gate"), cmp_pos, cmp_w1, cmp_w2,
                        rel_bias[:, A_HEADS:A_HEADS + B_HEADS])
    y_c = stick_breaking_attention(proj("c_q"), proj("c_k"), proj("c_v"))
    bsz, s, _ = u.shape
    gates = jax.nn.sigmoid(proj("merge_gate")).reshape(bsz, s, N_BRANCH, D_MODEL)
    merged = (gates[:, :, 0] * (y_a @ w_branch[0]) + gates[:, :, 1] * (y_b @ w_branch[1])
              + gates[:, :, 2] * (y_c @ w_branch[2]))
    return merged @ w_out


def sandwich(x, mod, i, ln_pre, ln_post, fn, res_w):
    shift, scale, gate = mod[:, i, 0][:, None, :], mod[:, i, 1][:, None, :], mod[:, i, 2][:, None, :]
    h = rms_norm(x, ln_pre[i]) * (1 + scale) + shift
    return x + res_w * gate * rms_norm(fn(h), ln_post[i])


def setup_inputs(seed: int = 0) -> dict:
    key = jax.random.key(seed)
    ks = jax.random.split(key, 20)
    f32 = jnp.float32

    def nrm(k, shape, scale):
        return jax.random.normal(k, shape, f32) * scale

    D = D_MODEL
    return {
        "x": nrm(ks[0], (BATCH, SEQ, D), 1.0),
        "c": nrm(ks[1], (BATCH, D), 1.0),
        "rel_bias": nrm(ks[2], (REL_BUCKETS, A_HEADS + B_HEADS), 0.5),
        "ada_w": nrm(ks[3], (DEPTH, D, N_SUB * 3 * D), 0.5 * D ** -0.5),
        "ada_b": nrm(ks[4], (DEPTH, N_SUB * 3 * D), 0.02),
        "ln_pre": 1.0 + nrm(ks[5], (DEPTH, N_SUB, D), 0.02),
        "ln_post": 1.0 + nrm(ks[6], (DEPTH, N_SUB, D), 0.02),
        "ffn_w_gate": nrm(ks[7], (DEPTH, 2, D, D_FF), D ** -0.5),
        "ffn_w_up": nrm(ks[8], (DEPTH, 2, D, D_FF), D ** -0.5),
        "ffn_w_down": nrm(ks[9], (DEPTH, 2, D_FF, D), D_FF ** -0.5),
        "w_in": nrm(ks[10], (DEPTH, D, IN_WIDTH), D ** -0.5),
        "attn_sinks": nrm(ks[11], (DEPTH, A_HEADS), 1.0),
        "cmp_pos": nrm(ks[12], (DEPTH, 2, CMP_BLOCK, HEAD_DIM), 0.1),
        "cmp_w1": nrm(ks[13], (DEPTH, 2, CMP_BLOCK * HEAD_DIM, CMP_HIDDEN), (CMP_BLOCK * HEAD_DIM) ** -0.5),
        "cmp_w2": nrm(ks[14], (DEPTH, 2, CMP_HIDDEN, HEAD_DIM), CMP_HIDDEN ** -0.5),
        "w_branch": nrm(ks[15], (DEPTH, N_BRANCH, BRANCH_WIDTH, D), BRANCH_WIDTH ** -0.5),
        "w_out": nrm(ks[16], (DEPTH, D, D), D ** -0.5),
    }


def reference(x, c, rel_bias, ada_w, ada_b, ln_pre, ln_post, ffn_w_gate, ffn_w_up, ffn_w_down,
              w_in, attn_sinks, cmp_pos, cmp_w1, cmp_w2, w_branch, w_out):
    bsz = x.shape[0]
    c_act = jax.nn.silu(c)
    for l in range(DEPTH):
        mod = (c_act @ ada_w[l] + ada_b[l]).reshape(bsz, N_SUB, 3, D_MODEL)
        x = sandwich(x, mod, 0, ln_pre[l], ln_post[l],
                     lambda h: swiglu(h, ffn_w_gate[l, 0], ffn_w_up[l, 0], ffn_w_down[l, 0]), FFN_RES)
        x = sandwich(x, mod, 1, ln_pre[l], ln_post[l],
                     lambda h: token_mixing(h, w_in[l], attn_sinks[l], cmp_pos[l], cmp_w1[l], cmp_w2[l],
                                            w_branch[l], w_out[l], rel_bias), 1.0)
        x = sandwich(x, mod, 2, ln_pre[l], ln_post[l],
                     lambda h: swiglu(h, ffn_w_gate[l, 1], ffn_w_up[l, 1], ffn_w_down[l, 1]), FFN_RES)
    return x
```

```python
import functools
import math

import numpy as np
import jax
import jax.numpy as jnp
from jax import lax
from jax.experimental import pallas as pl
from jax.experimental.pallas import tpu as pltpu

F32 = jnp.float32
BF16 = jnp.bfloat16

HEAD_DIM = 64
A_HEADS, A_KV, A_WINDOW = 8, 2, 128
B_HEADS, B_KV, B_WINDOW = 8, 2, 512
CMP_BLOCK, CMP_STRIDE, CMP_HIDDEN = 32, 16, 128
SEL_BLOCK, SEL_TOPN = 64, 16
C_HEADS = 8
BRANCH_WIDTH = 512
N_BRANCH = 3
REL_BUCKETS, REL_MAX_DIST = 32, 128
N_SUB = 3
EPS = 1e-6
FFN_RES = 0.5
NEG = -1e30

LANES = 128
QB = 128
VMEM_LIMIT = 56 * 1024 * 1024

ATT_COLS = {
    "a_q": 0, "b_q": 512, "c_q": 1024, "c_k": 1536, "c_v": 2048,
    "a_k": 2560, "a_v": 2688, "b_k_cmp": 2816, "b_v_cmp": 2944,
    "b_k_slc": 3072, "b_v_slc": 3200, "b_k_win": 3328, "b_v_win": 3456,
}
ATT_W = 3584
MERGE_W = 3072
GATE_W = MERGE_W + LANES


def _src_layout(d_model):
    widths = [
        ("a_q", A_HEADS * HEAD_DIM), ("a_k", A_KV * HEAD_DIM), ("a_v", A_KV * HEAD_DIM),
        ("b_q", B_HEADS * HEAD_DIM),
        ("b_k_cmp", B_KV * HEAD_DIM), ("b_v_cmp", B_KV * HEAD_DIM),
        ("b_k_slc", B_KV * HEAD_DIM), ("b_v_slc", B_KV * HEAD_DIM),
        ("b_k_win", B_KV * HEAD_DIM), ("b_v_win", B_KV * HEAD_DIM),
        ("b_gate", 3 * B_HEADS),
        ("c_q", C_HEADS * HEAD_DIM), ("c_k", C_HEADS * HEAD_DIM), ("c_v", C_HEADS * HEAD_DIM),
        ("merge_gate", N_BRANCH * d_model),
    ]
    out, off = {}, 0
    for name, w in widths:
        out[name] = (off, off + w)
        off += w
    return out


def _cparams(sem):
    return pltpu.CompilerParams(dimension_semantics=sem, vmem_limit_bytes=VMEM_LIMIT)


def _resident(shape, index_map):
    return pl.BlockSpec(shape, index_map, pipeline_mode=pl.Buffered(1))


def _rms(x, gain):
    return x * lax.rsqrt(jnp.mean(x * x, axis=-1, keepdims=True) + EPS) * gain


def _modulated(x, mod_ref, sub, gain):
    shift = mod_ref[0, 3 * sub:3 * sub + 1, :]
    scale = mod_ref[0, 3 * sub + 1:3 * sub + 2, :]
    return _rms(x, gain) * (1.0 + scale) + shift


def _dot(a, b):
    return jnp.dot(a, b, preferred_element_type=F32)


def _dot_nt(a, b):
    return lax.dot_general(a, b, (((1,), (1,)), ((), ())), preferred_element_type=F32)


def _split3(x):
    hi = x.astype(BF16)
    r1 = x - hi.astype(F32)
    mid = r1.astype(BF16)
    lo = (r1 - mid.astype(F32)).astype(BF16)
    return hi, mid, lo


def _dot_exact(x, sel):
    hi, mid, lo = _split3(x)
    return _dot(hi, sel) + _dot(mid, sel) + _dot(lo, sel)


def _lane_half(shape):
    return (lax.broadcasted_iota(jnp.int32, shape, 1) % LANES) // HEAD_DIM


def _stack_group_queries(q, group, heads_per_group):
    half = _lane_half((QB, LANES))
    parts = []
    for hp in range(heads_per_group):
        h = group * heads_per_group + hp
        tile = q[:, (h // 2) * LANES:(h // 2 + 1) * LANES]
        if h % 2 != group:
            tile = pltpu.roll(tile, HEAD_DIM, 1)
        parts.append(jnp.where(half == group, tile, 0.0))
    scale = 1.0 / math.sqrt(HEAD_DIM)
    return (jnp.concatenate(parts, axis=0) * scale).astype(BF16)


def _unstack_group_outputs(pv, group, heads_per_group):
    half = _lane_half((QB, LANES))
    tiles = []
    for pair in range(heads_per_group // 2):
        even = pv[(2 * pair) * QB:(2 * pair + 1) * QB]
        odd = pv[(2 * pair + 1) * QB:(2 * pair + 2) * QB]
        if group != 0:
            even = pltpu.roll(even, HEAD_DIM, 1)
        if group != 1:
            odd = pltpu.roll(odd, HEAD_DIM, 1)
        tiles.append(jnp.where(half == 0, even, odd))
    return jnp.concatenate(tiles, axis=1)


def _mod_kernel(c_ref, w_ref, b_ref, o_ref):
    c = c_ref[...]
    act = (c * jax.nn.sigmoid(c)).astype(BF16)
    o_ref[0] = _dot(act, w_ref[0].astype(BF16)) + b_ref[0]


def _modulation(c, ada_w, ada_b):
    depth, d, n = ada_w.shape
    bsz = c.shape[0]
    tn = n // 4
    out = pl.pallas_call(
        _mod_kernel,
        grid=(depth, n // tn),
        in_specs=[
            pl.BlockSpec((bsz, d), lambda l, j: (0, 0)),
            pl.BlockSpec((1, d, tn), lambda l, j: (l, 0, j)),
            pl.BlockSpec((1, 1, tn), lambda l, j: (l, 0, j)),
        ],
        out_specs=pl.BlockSpec((1, bsz, tn), lambda l, j: (l, 0, j)),
        out_shape=jax.ShapeDtypeStruct((depth, bsz, n), F32),
        compiler_params=_cparams(("arbitrary", "arbitrary")),
        name="adaln_mod",
    )(c, ada_w, ada_b.reshape(depth, 1, n))
    return out.reshape(depth, bsz, N_SUB * 3, d)


def _ffn_kernel(x_ref, mod_ref, lnpre_ref, lnpost_ref, wg_ref, wu_ref, wd_ref, o_ref, acc_ref, *, sub, tf):
    x = x_ref[0]
    h = _modulated(x, mod_ref, sub, lnpre_ref[...]).astype(BF16)
    d_ff = wg_ref.shape[1]
    for j in range(d_ff // tf):
        g = _dot(h, wg_ref[:, j * tf:(j + 1) * tf])
        u = _dot(h, wu_ref[:, j * tf:(j + 1) * tf])
        a = (g * jax.nn.sigmoid(g) * u).astype(BF16)
        part = _dot(a, wd_ref[j * tf:(j + 1) * tf, :])
        if j == 0:
            acc_ref[...] = part
        else:
            acc_ref[...] += part
    gate = mod_ref[0, 3 * sub + 2:3 * sub + 3, :]
    o_ref[0] = x + FFN_RES * gate * _rms(acc_ref[...], lnpost_ref[...])


def _ffn(x, mod, ln_pre, ln_post, wg, wu, wd, sub, tm=512, tf=256):
    bsz, s, d = x.shape
    d_ff = wg.shape[1]
    row = lambda b, i: (b, i, 0)
    const2 = lambda b, i: (0, 0)
    return pl.pallas_call(
        functools.partial(_ffn_kernel, sub=sub, tf=tf),
        grid=(bsz, s // tm),
        in_specs=[
            pl.BlockSpec((1, tm, d), row),
            pl.BlockSpec((1, N_SUB * 3, d), lambda b, i: (b, 0, 0)),
            pl.BlockSpec((1, d), const2),
            pl.BlockSpec((1, d), const2),
            _resident((d, d_ff), const2),
            _resident((d, d_ff), const2),
            _resident((d_ff, d), const2),
        ],
        out_specs=pl.BlockSpec((1, tm, d), row),
        out_shape=jax.ShapeDtypeStruct(x.shape, F32),
        scratch_shapes=[pltpu.VMEM((tm, d), F32)],
        compiler_params=_cparams(("arbitrary", "arbitrary")),
        name="ffn",
    )(x, mod, ln_pre, ln_post, wg, wu, wd)


def _inproj_kernel(x_ref, mod_ref, lnpre_ref, watt_ref, wgate_ref, att_ref, gate_ref, *, att_chunk, gate_chunk):
    h = _modulated(x_ref[0], mod_ref, 1, lnpre_ref[...]).astype(BF16)
    for c0 in range(0, ATT_W, att_chunk):
        att_ref[0, :, c0:c0 + att_chunk] = _dot(h, watt_ref[:, c0:c0 + att_chunk]).astype(BF16)
    for c0 in range(0, GATE_W, gate_chunk):
        gate_ref[0, :, c0:c0 + gate_chunk] = _dot(h, wgate_ref[:, c0:c0 + gate_chunk])


def _inproj(x, mod, ln_pre, w_att, w_gate, tm=256):
    bsz, s, d = x.shape
    row = lambda b, i: (b, i, 0)
    const2 = lambda b, i: (0, 0)
    return pl.pallas_call(
        functools.partial(_inproj_kernel, att_chunk=512, gate_chunk=640),
        grid=(bsz, s // tm),
        in_specs=[
            pl.BlockSpec((1, tm, d), row),
            pl.BlockSpec((1, N_SUB * 3, d), lambda b, i: (b, 0, 0)),
            pl.BlockSpec((1, d), const2),
            _resident((d, ATT_W), const2),
            _resident((d, GATE_W), const2),
        ],
        out_specs=[pl.BlockSpec((1, tm, ATT_W), row), pl.BlockSpec((1, tm, GATE_W), row)],
        out_shape=[jax.ShapeDtypeStruct((bsz, s, ATT_W), BF16), jax.ShapeDtypeStruct((bsz, s, GATE_W), F32)],
        compiler_params=_cparams(("arbitrary", "arbitrary")),
        name="inproj",
    )(x, mod, ln_pre, w_att, w_gate)


def _band_kernel(*refs, window, has_sink):
    if has_sink:
        q_ref, k_ref, v_ref, bias_ref, sink_ref, o_ref = refs
    else:
        q_ref, k_ref, v_ref, bias_ref, o_ref = refs
        sink_ref = None
    i = pl.program_id(1)
    nkb = window // QB + 1
    span = nkb * QB
    starts = [jnp.maximum(i - (nkb - 1) + kb, 0) * QB for kb in range(nkb)]
    kspan = jnp.concatenate([k_ref[0, pl.ds(pl.multiple_of(st, QB), QB), :] for st in starts], axis=0)
    vspan = jnp.concatenate([v_ref[0, pl.ds(pl.multiple_of(st, QB), QB), :] for st in starts], axis=0)
    q = q_ref[0].astype(F32)
    n_groups = bias_ref.shape[0]
    hpg = bias_ref.shape[1] // QB
    jk = lax.broadcasted_iota(jnp.int32, (1, span), 1)
    in_seq = jk >= (nkb - 1 - i) * QB
    outs = []
    for g in range(n_groups):
        qs = _stack_group_queries(q, g, hpg)
        logits = _dot_nt(qs, kspan) + bias_ref[g]
        logits = jnp.where(in_seq, logits, NEG)
        m = jnp.max(logits, axis=-1, keepdims=True)
        if has_sink:
            sk = sink_ref[g]
            m = jnp.maximum(m, sk)
            e = jnp.exp(logits - m)
            denom = jnp.sum(e, axis=-1, keepdims=True) + jnp.exp(sk - m)
        else:
            e = jnp.exp(logits - m)
            denom = jnp.sum(e, axis=-1, keepdims=True)
        p = (e / denom).astype(BF16)
        outs.append(_unstack_group_outputs(_dot(p, vspan), g, hpg))
    o_ref[0] = jnp.concatenate(outs, axis=1).astype(BF16)


def _band_attention(att, q_col, k_col, v_col, bias, sink, window):
    bsz, s, _ = att.shape
    n_groups, rows, span = bias.shape
    width = n_groups * (rows // QB) * HEAD_DIM
    in_specs = [
        pl.BlockSpec((1, QB, width), lambda b, i: (b, i, q_col // width)),
        pl.BlockSpec((1, s, LANES), lambda b, i: (b, 0, k_col // LANES)),
        pl.BlockSpec((1, s, LANES), lambda b, i: (b, 0, v_col // LANES)),
        _resident((n_groups, rows, span), lambda b, i: (0, 0, 0)),
    ]
    args = [att, att, att, bias]
    if sink is not None:
        in_specs.append(pl.BlockSpec((n_groups, rows, 1), lambda b, i: (0, 0, 0)))
        args.append(sink)
    return pl.pallas_call(
        functools.partial(_band_kernel, window=window, has_sink=sink is not None),
        grid=(bsz, s // QB),
        in_specs=in_specs,
        out_specs=pl.BlockSpec((1, QB, width), lambda b, i: (b, i, 0)),
        out_shape=jax.ShapeDtypeStruct((bsz, s, width), BF16),
        compiler_params=_cparams(("arbitrary", "arbitrary")),
        name=f"band{window}",
    )(*args)


def _compress_kernel(rk_ref, rv_ref, pos_ref, w1_ref, w2_ref, kc_ref, vc_ref):
    n_rows = rk_ref.shape[2]
    last = lax.broadcasted_iota(jnp.int32, (n_rows, 1), 0) == n_rows - 1
    for which, (r_ref, o_ref) in enumerate(((rk_ref, kc_ref), (rv_ref, vc_ref))):
        halves = []
        for g in range(B_KV):
            r = r_ref[0, g].astype(F32)
            lo = (r + pos_ref[which, 0]).astype(BF16)
            hi = (r + pos_ref[which, 1]).astype(BF16)
            hid = _dot(lo, w1_ref[which, 0]) + pltpu.roll(_dot(hi, w1_ref[which, 1]), n_rows - 1, 0)
            out = _dot(jax.nn.gelu(hid).astype(BF16), w2_ref[which])
            halves.append(jnp.where(last, 0.0, out))
        o_ref[0] = jnp.concatenate(halves, axis=1).astype(BF16)


def _compress(rk, rv, pos, w1, w2):
    bsz, g, n_rows, width = rk.shape
    r_spec = pl.BlockSpec((1, g, n_rows, width), lambda b: (b, 0, 0, 0))
    o_spec = pl.BlockSpec((1, n_rows, g * HEAD_DIM), lambda b: (b, 0, 0))
    o_shape = jax.ShapeDtypeStruct((bsz, n_rows, g * HEAD_DIM), BF16)
    return pl.pallas_call(
        _compress_kernel,
        grid=(bsz,),
        in_specs=[
            r_spec, r_spec,
            pl.BlockSpec(pos.shape, lambda b: (0, 0, 0, 0)),
            pl.BlockSpec(w1.shape, lambda b: (0, 0, 0, 0)),
            pl.BlockSpec(w2.shape, lambda b: (0, 0, 0)),
        ],
        out_specs=[o_spec, o_spec],
        out_shape=[o_shape, o_shape],
        compiler_params=_cparams(("arbitrary",)),
        name="nsa_compress",
    )(rk, rv, pos, w1, w2)


def _cmpsel_kernel(q_ref, kc_ref, vc_ref, ov_ref, o_ref, sel_ref):
    i = pl.program_id(1)
    q = q_ref[0].astype(F32)
    kc = kc_ref[0]
    vc = vc_ref[0]
    n_cmp_rows = kc.shape[0]
    hpg = B_HEADS // B_KV
    rows = hpg * QB
    t_stack = i * QB + lax.broadcasted_iota(jnp.int32, (rows, 1), 0) % QB
    n_idx = lax.broadcasted_iota(jnp.int32, (1, n_cmp_rows), 1)
    cmp_ok = (n_idx * CMP_STRIDE + CMP_BLOCK - 1 <= t_stack) & (n_idx < n_cmp_rows - 1)
    t = i * QB + lax.broadcasted_iota(jnp.int32, (QB, 1), 0)
    lane = lax.broadcasted_iota(jnp.int32, (1, LANES), 1)
    n_sel = (n_cmp_rows * CMP_STRIDE) // SEL_BLOCK
    cur = t // SEL_BLOCK
    forced = (lane == 0) | (lane == cur) | (lane == cur - 1)
    future = lane * SEL_BLOCK > t
    outs = []
    sel_all = jnp.zeros((QB, LANES), F32)
    for g in range(B_KV):
        qs = _stack_group_queries(q, g, hpg)
        logits = jnp.where(cmp_ok, _dot_nt(qs, kc), NEG)
        m = jnp.max(logits, axis=-1, keepdims=True)
        e = jnp.where(cmp_ok, jnp.exp(logits - m), 0.0)
        ssum = jnp.sum(e, axis=-1, keepdims=True)
        p = e / jnp.where(ssum > 0, ssum, 1.0)
        outs.append(_unstack_group_outputs(_dot(p.astype(BF16), vc), g, hpg))
        p_heads = p[0:QB]
        for hp in range(1, hpg):
            p_heads = p_heads + p[hp * QB:(hp + 1) * QB]
        imp = _dot_exact(p_heads, ov_ref[...])
        prio = jnp.where(forced, jnp.inf, jnp.where(future, -jnp.inf, imp))
        prio = jnp.where(lane < n_sel, prio, -jnp.inf)
        rank = jnp.zeros((QB, LANES), F32)
        for c in range(n_sel):
            col = prio[:, c:c + 1]
            ahead = (col > prio) | ((col == prio) & (lane > c))
            rank = rank + jnp.where(ahead, 1.0, 0.0)
        sel = jnp.where((rank < min(SEL_TOPN, n_sel)) & (lane < n_sel), 1.0, 0.0)
        if g > 0:
            sel = pltpu.roll(sel, g * n_sel, 1)
        sel_all = sel_all + sel
    o_ref[0] = jnp.concatenate(outs, axis=1).astype(BF16)
    sel_ref[0] = sel_all.astype(BF16)


def _cmpsel(att, kc, vc, overlap):
    bsz, s, _ = att.shape
    n_rows = kc.shape[1]
    width = B_HEADS * HEAD_DIM
    return pl.pallas_call(
        _cmpsel_kernel,
        grid=(bsz, s // QB),
        in_specs=[
            pl.BlockSpec((1, QB, width), lambda b, i: (b, i, ATT_COLS["b_q"] // width)),
            pl.BlockSpec((1, n_rows, LANES), lambda b, i: (b, 0, 0)),
            pl.BlockSpec((1, n_rows, LANES), lambda b, i: (b, 0, 0)),
            pl.BlockSpec(overlap.shape, lambda b, i: (0, 0)),
        ],
        out_specs=[pl.BlockSpec((1, QB, width), lambda b, i: (b, i, 0)),
                   pl.BlockSpec((1, QB, LANES), lambda b, i: (b, i, 0))],
        out_shape=[jax.ShapeDtypeStruct((bsz, s, width), BF16), jax.ShapeDtypeStruct((bsz, s, LANES), BF16)],
        compiler_params=_cparams(("arbitrary", "arbitrary")),
        name="nsa_cmpsel",
    )(att, kc, vc, overlap)


def _slc_kernel(q_ref, k_ref, v_ref, sel_ref, bias_ref, far_ref, o_ref, *, n_sel):
    i = pl.program_id(1)
    hpg = B_HEADS // B_KV
    rows = hpg * QB
    q = q_ref[0].astype(F32)
    sel = sel_ref[0]
    sel_stack = jnp.concatenate([sel] * hpg, axis=0)
    sel_row = lax.broadcasted_iota(jnp.int32, (LANES, QB), 0)
    key_blk = lax.broadcasted_iota(jnp.int32, (LANES, QB), 1) // SEL_BLOCK
    iq = lax.broadcasted_iota(jnp.int32, (rows, QB), 0) % QB
    jk = lax.broadcasted_iota(jnp.int32, (rows, QB), 1)
    outs = []
    for g in range(B_KV):
        qs = _stack_group_queries(q, g, hpg)

        def tile_step(j, carry, bias, causal, g=g, qs=qs):
            m, l, acc = carry
            st = pl.multiple_of(j * QB, QB)
            kt = k_ref[0, pl.ds(st, QB), :]
            vt = v_ref[0, pl.ds(st, QB), :]
            expand = jnp.where(sel_row == g * n_sel + j * (QB // SEL_BLOCK) + key_blk, 1.0, 0.0).astype(BF16)
            ok = _dot(sel_stack, expand) > 0.5
            if causal is not None:
                ok = ok & causal
            logits = jnp.where(ok, _dot_nt(qs, kt) + bias, NEG)
            m_new = jnp.maximum(m, jnp.max(logits, axis=-1, keepdims=True))
            alpha = jnp.exp(m - m_new)
            e = jnp.where(ok, jnp.exp(logits - m_new), 0.0)
            l = alpha * l + jnp.sum(e, axis=-1, keepdims=True)
            acc = alpha * acc + _dot(e.astype(BF16), vt)
            return m_new, l, acc

        init = (jnp.full((rows, 1), NEG, F32), jnp.zeros((rows, 1), F32), jnp.zeros((rows, LANES), F32))
        far_bias = far_ref[g]
        carry = lax.fori_loop(0, jnp.maximum(i - 1, 0),
                              lambda j, c, step=tile_step, fb=far_bias: step(j, c, fb, None), init)
        carry = tile_step(jnp.maximum(i - 1, 0), carry, bias_ref[g, :, 0:QB], (i >= 1) & (jk >= 0))
        m, l, acc = tile_step(i, carry, bias_ref[g, :, QB:2 * QB], jk <= iq)
        outs.append(_unstack_group_outputs(acc / jnp.where(l > 0, l, 1.0), g, hpg))
    o_ref[0] = jnp.concatenate(outs, axis=1).astype(BF16)


def _slc_attention(att, sel, bias_near, bias_far, n_sel):
    bsz, s, _ = att.shape
    width = B_HEADS * HEAD_DIM
    n_groups, rows, _ = bias_near.shape
    return pl.pallas_call(
        functools.partial(_slc_kernel, n_sel=n_sel),
        grid=(bsz, s // QB),
        in_specs=[
            pl.BlockSpec((1, QB, width), lambda b, i: (b, i, ATT_COLS["b_q"] // width)),
            pl.BlockSpec((1, s, LANES), lambda b, i: (b, 0, ATT_COLS["b_k_slc"] // LANES)),
            pl.BlockSpec((1, s, LANES), lambda b, i: (b, 0, ATT_COLS["b_v_slc"] // LANES)),
            pl.BlockSpec((1, QB, LANES), lambda b, i: (b, i, 0)),
            _resident((n_groups, rows, 2 * QB), lambda b, i: (0, 0, 0)),
            pl.BlockSpec((n_groups, rows, 1), lambda b, i: (0, 0, 0)),
        ],
        out_specs=pl.BlockSpec((1, QB, width), lambda b, i: (b, i, 0)),
        out_shape=jax.ShapeDtypeStruct((bsz, s, width), BF16),
        compiler_params=_cparams(("arbitrary", "arbitrary")),
        name="nsa_slc",
    )(att, att, att, sel, bias_near, bias_far)


def _stick_kernel(q_ref, k_ref, v_ref, o_ref):
    i = pl.program_id(2)
    rows = 2 * QB
    half = _lane_half((QB, LANES))
    q = q_ref[0].astype(F32) * (1.0 / math.sqrt(HEAD_DIM))
    qs = jnp.concatenate([jnp.where(half == 0, q, 0.0), jnp.where(half == 1, q, 0.0)], axis=0).astype(BF16)
    iq = lax.broadcasted_iota(jnp.int32, (rows, QB), 0) % QB
    jk = lax.broadcasted_iota(jnp.int32, (rows, QB), 1)
    later_than = jnp.where(lax.broadcasted_iota(jnp.int32, (QB, QB), 0)
                           > lax.broadcasted_iota(jnp.int32, (QB, QB), 1), 1.0, 0.0).astype(BF16)

    def tile_step(step, carry):
        tail, acc = carry
        j = i - step
        st = pl.multiple_of(j * QB, QB)
        kt = k_ref[0, pl.ds(st, QB), :]
        vt = v_ref[0, pl.ds(st, QB), :]
        z = _dot_nt(qs, kt)
        strict = (j * QB + jk) < (i * QB + iq)
        log_beta = jnp.minimum(z, 0.0) - jnp.log1p(jnp.exp(-jnp.abs(z)))
        log_keep = jnp.where(strict, log_beta - z, 0.0)
        later = _dot_exact(log_keep, later_than) + tail
        w = jnp.where(strict, jnp.exp(log_beta + later), 0.0)
        acc = acc + _dot(w.astype(BF16), vt)
        tail = tail + jnp.sum(log_keep, axis=-1, keepdims=True)
        return tail, acc

    init = (jnp.zeros((rows, 1), F32), jnp.zeros((rows, LANES), F32))
    _, acc = lax.fori_loop(0, i + 1, tile_step, init)
    o_ref[0] = jnp.where(half == 0, acc[0:QB], acc[QB:rows]).astype(BF16)


def _stick_attention(att):
    bsz, s, _ = att.shape
    pairs = C_HEADS // 2
    col = lambda name: ATT_COLS[name] // LANES
    return pl.pallas_call(
        _stick_kernel,
        grid=(bsz, pairs, s // QB),
        in_specs=[
            pl.BlockSpec((1, QB, LANES), lambda b, h, i: (b, i, col("c_q") + h)),
            pl.BlockSpec((1, s, LANES), lambda b, h, i: (b, 0, col("c_k") + h)),
            pl.BlockSpec((1, s, LANES), lambda b, h, i: (b, 0, col("c_v") + h)),
        ],
        out_specs=pl.BlockSpec((1, QB, LANES), lambda b, h, i: (b, i, h)),
        out_shape=jax.ShapeDtypeStruct((bsz, s, C_HEADS * HEAD_DIM), BF16),
        compiler_params=_cparams(("arbitrary", "arbitrary", "arbitrary")),
        name="stick",
    )(att, att, att)


def _merge_kernel(x_ref, mod_ref, lnpost_ref, ya_ref, oc_ref, os_ref, ow_ref, yc_ref, mg_ref, bg_ref,
                  wb_ref, wo_ref, o_ref):
    d = x_ref.shape[2]
    bw = ya_ref.shape[2]
    head_gate = jax.nn.sigmoid(bg_ref[0])
    gate_row = lax.broadcasted_iota(jnp.int32, (LANES, bw), 0)
    head_of_col = lax.broadcasted_iota(jnp.int32, (LANES, bw), 1) // HEAD_DIM
    y_b = jnp.zeros(ya_ref.shape[1:], F32)
    for br, ref in enumerate((oc_ref, os_ref, ow_ref)):
        expand = jnp.where(gate_row == br * B_HEADS + head_of_col, 1.0, 0.0).astype(BF16)
        y_b = y_b + _dot_exact(head_gate, expand) * ref[0].astype(F32)
    merged = jnp.zeros((x_ref.shape[1], d), F32)
    for br, y in enumerate((ya_ref[0], y_b.astype(BF16), yc_ref[0])):
        merged = merged + jax.nn.sigmoid(mg_ref[0, :, br * d:(br + 1) * d]) * _dot(y, wb_ref[br])
    t = _dot(merged.astype(BF16), wo_ref[...])
    gate = mod_ref[0, 5:6, :]
    o_ref[0] = x_ref[0] + gate * _rms(t, lnpost_ref[...])


def _merge(x, mod, ln_post, y_a, o_cmp, o_slc, o_win, y_c, gates, w_branch, w_out, tm=256):
    bsz, s, d = x.shape
    bw = y_a.shape[2]
    row = lambda b, i: (b, i, 0)
    const2 = lambda b, i: (0, 0)
    branch = pl.BlockSpec((1, tm, bw), row)
    return pl.pallas_call(
        _merge_kernel,
        grid=(bsz, s // tm),
        in_specs=[
            pl.BlockSpec((1, tm, d), row),
            pl.BlockSpec((1, N_SUB * 3, d), lambda b, i: (b, 0, 0)),
            pl.BlockSpec((1, d), const2),
            branch, branch, branch, branch, branch,
            pl.BlockSpec((1, tm, MERGE_W), row),
            pl.BlockSpec((1, tm, LANES), lambda b, i: (b, i, MERGE_W // LANES)),
            _resident(w_branch.shape, lambda b, i: (0, 0, 0)),
            _resident(w_out.shape, const2),
        ],
        out_specs=pl.BlockSpec((1, tm, d), row),
        out_shape=jax.ShapeDtypeStruct(x.shape, F32),
        compiler_params=_cparams(("arbitrary", "arbitrary")),
        name="merge",
    )(x, mod, ln_post, y_a, o_cmp, o_slc, o_win, y_c, gates, gates, w_branch, w_out)


def _t5_bucket(dist):
    max_exact = REL_BUCKETS // 2
    d = jnp.maximum(dist, 0)
    ratio = jnp.log(jnp.maximum(d, 1).astype(F32) / max_exact) / math.log(REL_MAX_DIST / max_exact)
    large = jnp.minimum(max_exact + (ratio * (REL_BUCKETS - max_exact)).astype(jnp.int32), REL_BUCKETS - 1)
    return jnp.where(d < max_exact, d, large)


def _band_bias(table, window, n_groups, mask_band=True):
    span = window + QB
    dist = window + jnp.arange(QB)[:, None] - jnp.arange(span)[None, :]
    bias = jnp.moveaxis(table[_t5_bucket(dist)], -1, 0).astype(F32)
    if mask_band:
        bias = jnp.where(((dist >= 0) & (dist < window))[None], bias, NEG)
    return bias.reshape(n_groups, -1, span)


def _far_bucket_is_constant():
    d = np.arange(QB + 1, 1 << 16).astype(np.float32)
    max_exact = REL_BUCKETS // 2
    ratio = np.log(d / max_exact) / math.log(REL_MAX_DIST / max_exact)
    return bool(np.all(max_exact + (ratio * (REL_BUCKETS - max_exact)).astype(np.int32) >= REL_BUCKETS - 1))


def _overlap_matrix(n_rows, n_sel):
    cstart = np.arange(n_rows)[:, None] * CMP_STRIDE
    sstart = np.arange(LANES)[None, :] * SEL_BLOCK
    ov = (cstart < sstart + SEL_BLOCK) & (cstart + CMP_BLOCK > sstart)
    ov &= (np.arange(n_rows)[:, None] < n_rows - 1) & (np.arange(LANES)[None, :] < n_sel)
    return jnp.asarray(ov, dtype=BF16)


def _layer_weights(w_in_l, d):
    src = _src_layout(d)
    order = sorted(ATT_COLS, key=ATT_COLS.get)
    w_att = jnp.concatenate([w_in_l[:, src[n][0]:src[n][1]] for n in order], axis=1).astype(BF16)
    g0, g1 = src["b_gate"]
    m0, m1 = src["merge_gate"]
    pad = jnp.zeros((d, LANES - (g1 - g0)), w_in_l.dtype)
    w_gate = jnp.concatenate([w_in_l[:, m0:m1], w_in_l[:, g0:g1], pad], axis=1).astype(BF16)
    return w_att, w_gate


def _compress_rows(att, name):
    bsz, s, _ = att.shape
    c0 = ATT_COLS[name]
    kv = att[:, :, c0:c0 + B_KV * HEAD_DIM].reshape(bsz, s // CMP_STRIDE, CMP_STRIDE, B_KV, HEAD_DIM)
    return kv.transpose(0, 3, 1, 2, 4).reshape(bsz, B_KV, s // CMP_STRIDE, CMP_STRIDE * HEAD_DIM)


def kernel(x, c, rel_bias, ada_w, ada_b, ln_pre, ln_post, ffn_w_gate, ffn_w_up, ffn_w_down, w_in, attn_sinks,
           cmp_pos, cmp_w1, cmp_w2, w_branch, w_out):
    bsz, s, d = x.shape
    depth = ada_w.shape[0]
    assert s % QB == 0 and s // CMP_STRIDE == LANES and CMP_BLOCK == 2 * CMP_STRIDE
    assert d == MERGE_W // N_BRANCH and A_WINDOW == QB and _far_bucket_is_constant()
    n_rows = s // CMP_STRIDE
    n_sel = s // SEL_BLOCK
    hpg_a, hpg_b = A_HEADS // A_KV, B_HEADS // B_KV

    mod = _modulation(c, ada_w, ada_b)
    table_a = rel_bias[:, :A_HEADS]
    table_b = rel_bias[:, A_HEADS:A_HEADS + B_HEADS]
    bias_a = _band_bias(table_a, A_WINDOW, A_KV)
    bias_w = _band_bias(table_b, B_WINDOW, B_KV)
    bias_near = _band_bias(table_b, QB, B_KV, mask_band=False)
    bias_far = jnp.repeat(table_b[REL_BUCKETS - 1].astype(F32), QB).reshape(B_KV, hpg_b * QB, 1)
    overlap = _overlap_matrix(n_rows, n_sel)
    half_w = CMP_STRIDE * HEAD_DIM

    for l in range(depth):
        mod_l = mod[l]
        x = _ffn(x, mod_l, ln_pre[l, 0:1], ln_post[l, 0:1], ffn_w_gate[l, 0].astype(BF16),
                 ffn_w_up[l, 0].astype(BF16), ffn_w_down[l, 0].astype(BF16), sub=0)

        w_att, w_gate = _layer_weights(w_in[l], d)
        att, gates = _inproj(x, mod_l, ln_pre[l, 1:2], w_att, w_gate)
        sink = jnp.repeat(attn_sinks[l].astype(F32), QB).reshape(A_KV, hpg_a * QB, 1)
        y_a = _band_attention(att, ATT_COLS["a_q"], ATT_COLS["a_k"], ATT_COLS["a_v"], bias_a, sink, A_WINDOW)
        kc, vc = _compress(
            _compress_rows(att, "b_k_cmp"), _compress_rows(att, "b_v_cmp"),
            cmp_pos[l].reshape(2, 2, 1, half_w).astype(F32),
            cmp_w1[l].reshape(2, 2, half_w, CMP_HIDDEN).astype(BF16), cmp_w2[l].astype(BF16))
        o_cmp, sel = _cmpsel(att, kc, vc, overlap)
        o_slc = _slc_attention(att, sel, bias_near, bias_far, n_sel)
        o_win = _band_attention(att, ATT_COLS["b_q"], ATT_COLS["b_k_win"], ATT_COLS["b_v_win"], bias_w, None,
                                B_WINDOW)
        y_c = _stick_attention(att)
        x = _merge(x, mod_l, ln_post[l, 1:2], y_a, o_cmp, o_slc, o_win, y_c, gates,
                   w_branch[l].astype(BF16), w_out[l].astype(BF16))

        x = _ffn(x, mod_l, ln_pre[l, 2:3], ln_post[l, 2:3], ffn_w_gate[l, 1].astype(BF16),
                 ffn_w_up[l, 1].astype(BF16), ffn_w_down[l, 1].astype(BF16), sub=2)
    return x
```

```python
import functools
import math

import numpy as np
import jax
import jax.numpy as jnp
from jax import lax
from jax.experimental import pallas as pl
from jax.experimental.pallas import tpu as pltpu

F32 = jnp.float32
BF16 = jnp.bfloat16

HEAD_DIM = 64
A_HEADS, A_KV, A_WINDOW = 8, 2, 128
B_HEADS, B_KV, B_WINDOW = 8, 2, 512
CMP_BLOCK, CMP_STRIDE, CMP_HIDDEN = 32, 16, 128
SEL_BLOCK, SEL_TOPN = 64, 16
C_HEADS = 8
BRANCH_WIDTH = 512
N_BRANCH = 3
REL_BUCKETS, REL_MAX_DIST = 32, 128
N_SUB = 3
EPS = 1e-6
FFN_RES = 0.5
NEG = -1e30

LANES = 128
QB = 128
VMEM_LIMIT = 56 * 1024 * 1024

ATT_COLS = {
    "a_q": 0, "b_q": 512, "c_q": 1024, "c_k": 1536, "c_v": 2048,
    "a_k": 2560, "a_v": 2688, "b_k_cmp": 2816, "b_v_cmp": 2944,
    "b_k_slc": 3072, "b_v_slc": 3200, "b_k_win": 3328, "b_v_win": 3456,
}
ATT_W = 3584
MERGE_W = 3072
GATE_W = MERGE_W + LANES


def _src_layout(d_model):
    widths = [
        ("a_q", A_HEADS * HEAD_DIM), ("a_k", A_KV * HEAD_DIM), ("a_v", A_KV * HEAD_DIM),
        ("b_q", B_HEADS * HEAD_DIM),
        ("b_k_cmp", B_KV * HEAD_DIM), ("b_v_cmp", B_KV * HEAD_DIM),
        ("b_k_slc", B_KV * HEAD_DIM), ("b_v_slc", B_KV * HEAD_DIM),
        ("b_k_win", B_KV * HEAD_DIM), ("b_v_win", B_KV * HEAD_DIM),
        ("b_gate", 3 * B_HEADS),
        ("c_q", C_HEADS * HEAD_DIM), ("c_k", C_HEADS * HEAD_DIM), ("c_v", C_HEADS * HEAD_DIM),
        ("merge_gate", N_BRANCH * d_model),
    ]
    out, off = {}, 0
    for name, w in widths:
        out[name] = (off, off + w)
        off += w
    return out


def _cparams(sem):
    return pltpu.CompilerParams(dimension_semantics=sem, vmem_limit_bytes=VMEM_LIMIT)


def _resident(shape, index_map):
    return pl.BlockSpec(shape, index_map, pipeline_mode=pl.Buffered(1))


def _rms(x, gain):
    return x * lax.rsqrt(jnp.mean(x * x, axis=-1, keepdims=True) + EPS) * gain


def _modulated(x, mod_ref, sub, gain):
    shift = mod_ref[0, 3 * sub:3 * sub + 1, :]
    scale = mod_ref[0, 3 * sub + 1:3 * sub + 2, :]
    return _rms(x, gain) * (1.0 + scale) + shift


def _dot(a, b):
    return jnp.dot(a, b, preferred_element_type=F32)


def _dot_nt(a, b):
    return lax.dot_general(a, b, (((1,), (1,)), ((), ())), preferred_element_type=F32)


def _split3(x):
    hi = x.astype(BF16)
    r1 = x - hi.astype(F32)
    mid = r1.astype(BF16)
    lo = (r1 - mid.astype(F32)).astype(BF16)
    return hi, mid, lo


def _dot_exact(x, sel):
    hi, mid, lo = _split3(x)
    return _dot(hi, sel) + _dot(mid, sel) + _dot(lo, sel)


def _lane_half(shape):
    return (lax.broadcasted_iota(jnp.int32, shape, 1) % LANES) // HEAD_DIM


def _stack_group_queries(q, group, heads_per_group):
    half = _lane_half((QB, LANES))
    parts = []
    for hp in range(heads_per_group):
        h = group * heads_per_group + hp
        tile = q[:, (h // 2) * LANES:(h // 2 + 1) * LANES]
        if h % 2 != group:
            tile = pltpu.roll(tile, HEAD_DIM, 1)
        parts.append(jnp.where(half == group, tile, 0.0))
    scale = 1.0 / math.sqrt(HEAD_DIM)
    return (jnp.concatenate(parts, axis=0) * scale).astype(BF16)


def _unstack_group_outputs(pv, group, heads_per_group):
    half = _lane_half((QB, LANES))
    tiles = []
    for pair in range(heads_per_group // 2):
        even = pv[(2 * pair) * QB:(2 * pair + 1) * QB]
        odd = pv[(2 * pair + 1) * QB:(2 * pair + 2) * QB]
        if group != 0:
            even = pltpu.roll(even, HEAD_DIM, 1)
        if group != 1:
            odd = pltpu.roll(odd, HEAD_DIM, 1)
        tiles.append(jnp.where(half == 0, even, odd))
    return jnp.concatenate(tiles, axis=1)


def _mod_kernel(c_ref, w_ref, b_ref, o_ref):
    c = c_ref[...]
    act = (c * jax.nn.sigmoid(c)).astype(BF16)
    o_ref[0] = _dot(act, w_ref[0].astype(BF16)) + b_ref[0]


def _modulation(c, ada_w, ada_b):
    depth, d, n = ada_w.shape
    bsz = c.shape[0]
    tn = n // 4
    out = pl.pallas_call(
        _mod_kernel,
        grid=(depth, n // tn),
        in_specs=[
            pl.BlockSpec((bsz, d), lambda l, j: (0, 0)),
            pl.BlockSpec((1, d, tn), lambda l, j: (l, 0, j)),
            pl.BlockSpec((1, 1, tn), lambda l, j: (l, 0, j)),
        ],
        out_specs=pl.BlockSpec((1, bsz, tn), lambda l, j: (l, 0, j)),
        out_shape=jax.ShapeDtypeStruct((depth, bsz, n), F32),
        compiler_params=_cparams(("arbitrary", "arbitrary")),
        name="adaln_mod",
    )(c, ada_w, ada_b.reshape(depth, 1, n))
    return out.reshape(depth, bsz, N_SUB * 3, d)


def _ffn_kernel(x_ref, mod_ref, lnpre_ref, lnpost_ref, wg_ref, wu_ref, wd_ref, o_ref, acc_ref, *, sub, tf):
    x = x_ref[0]
    h = _modulated(x, mod_ref, sub, lnpre_ref[...]).astype(BF16)
    d_ff = wg_ref.shape[1]
    for j in range(d_ff // tf):
        g = _dot(h, wg_ref[:, j * tf:(j + 1) * tf])
        u = _dot(h, wu_ref[:, j * tf:(j + 1) * tf])
        a = (g * jax.nn.sigmoid(g) * u).astype(BF16)
        part = _dot(a, wd_ref[j * tf:(j + 1) * tf, :])
        if j == 0:
            acc_ref[...] = part
        else:
            acc_ref[...] += part
    gate = mod_ref[0, 3 * sub + 2:3 * sub + 3, :]
    o_ref[0] = x + FFN_RES * gate * _rms(acc_ref[...], lnpost_ref[...])


def _ffn(x, mod, ln_pre, ln_post, wg, wu, wd, sub, tm=512, tf=256):
    bsz, s, d = x.shape
    d_ff = wg.shape[1]
    row = lambda b, i: (b, i, 0)
    const2 = lambda b, i: (0, 0)
    return pl.pallas_call(
        functools.partial(_ffn_kernel, sub=sub, tf=tf),
        grid=(bsz, s // tm),
        in_specs=[
            pl.BlockSpec((1, tm, d), row),
            pl.BlockSpec((1, N_SUB * 3, d), lambda b, i: (b, 0, 0)),
            pl.BlockSpec((1, d), const2),
            pl.BlockSpec((1, d), const2),
            _resident((d, d_ff), const2),
            _resident((d, d_ff), const2),
            _resident((d_ff, d), const2),
        ],
        out_specs=pl.BlockSpec((1, tm, d), row),
        out_shape=jax.ShapeDtypeStruct(x.shape, F32),
        scratch_shapes=[pltpu.VMEM((tm, d), F32)],
        compiler_params=_cparams(("arbitrary", "arbitrary")),
        name="ffn",
    )(x, mod, ln_pre, ln_post, wg, wu, wd)


def _inproj_kernel(x_ref, mod_ref, lnpre_ref, watt_ref, wgate_ref, att_ref, gate_ref, *, att_chunk, gate_chunk):
    h = _modulated(x_ref[0], mod_ref, 1, lnpre_ref[...]).astype(BF16)
    for c0 in range(0, ATT_W, att_chunk):
        att_ref[0, :, c0:c0 + att_chunk] = _dot(h, watt_ref[:, c0:c0 + att_chunk]).astype(BF16)
    for c0 in range(0, GATE_W, gate_chunk):
        gate_ref[0, :, c0:c0 + gate_chunk] = _dot(h, wgate_ref[:, c0:c0 + gate_chunk])


def _inproj(x, mod, ln_pre, w_att, w_gate, tm=256):
    bsz, s, d = x.shape
    row = lambda b, i: (b, i, 0)
    const2 = lambda b, i: (0, 0)
    return pl.pallas_call(
        functools.partial(_inproj_kernel, att_chunk=512, gate_chunk=640),
        grid=(bsz, s // tm),
        in_specs=[
            pl.BlockSpec((1, tm, d), row),
            pl.BlockSpec((1, N_SUB * 3, d), lambda b, i: (b, 0, 0)),
            pl.BlockSpec((1, d), const2),
            _resident((d, ATT_W), const2),
            _resident((d, GATE_W), const2),
        ],
        out_specs=[pl.BlockSpec((1, tm, ATT_W), row), pl.BlockSpec((1, tm, GATE_W), row)],
        out_shape=[jax.ShapeDtypeStruct((bsz, s, ATT_W), BF16), jax.ShapeDtypeStruct((bsz, s, GATE_W), F32)],
        compiler_params=_cparams(("arbitrary", "arbitrary")),
        name="inproj",
    )(x, mod, ln_pre, w_att, w_gate)


def _band_kernel(q_ref, k_ref, v_ref, bias_ref, o_ref, *, window, has_sink, q_tiles):
    nkb = window // QB + 1
    span = nkb * QB
    n_groups = bias_ref.shape[0]
    hpg = bias_ref.shape[1] // QB
    jk = lax.broadcasted_iota(jnp.int32, (1, span), 1)
    row0 = lax.broadcasted_iota(jnp.int32, (span, LANES), 0) == 0
    vhalf = _lane_half((span, LANES))
    chains = [(u, g) for u in range(q_tiles) for g in range(n_groups)]
    vspans, in_seqs, scores = [], [], []
    for u in range(q_tiles):
        i = pl.program_id(1) * q_tiles + u
        starts = [jnp.maximum(i - (nkb - 1) + kb, 0) * QB for kb in range(nkb)]
        kspan = jnp.concatenate([k_ref[0, pl.ds(pl.multiple_of(st, QB), QB), :] for st in starts], axis=0)
        vspan = jnp.concatenate([v_ref[0, pl.ds(pl.multiple_of(st, QB), QB), :] for st in starts], axis=0)
        in_seq = jk >= (nkb - 1 - i) * QB
        if has_sink:
            kspan = jnp.where(row0, jnp.zeros_like(kspan), kspan)
            vspan = jnp.where(row0, jnp.zeros_like(vspan), vspan)
            in_seq = in_seq | (jk == 0)
        vspans.append(vspan)
        in_seqs.append(in_seq)
        q = q_ref[0, u * QB:(u + 1) * QB, :].astype(F32)
        for g in range(n_groups):
            scores.append(_dot_nt(_stack_group_queries(q, g, hpg), kspan))
    weights = []
    for (u, g), z in zip(chains, scores):
        logits = jnp.where(in_seqs[u], z + bias_ref[g], NEG)
        weights.append(jnp.exp(logits - jnp.max(logits, axis=-1, keepdims=True)).astype(BF16))
    pvs = [_dot(e, jnp.where(vhalf == g, vspans[u], jnp.ones_like(vspans[u]))) for (u, g), e in zip(chains, weights)]
    for u in range(q_tiles):
        outs = []
        for g in range(n_groups):
            pv = pvs[u * n_groups + g]
            outs.append(_unstack_group_outputs(pv / pltpu.roll(pv, HEAD_DIM, 1), g, hpg))
        o_ref[0, u * QB:(u + 1) * QB, :] = jnp.concatenate(outs, axis=1).astype(BF16)


def _band_attention(att, q_col, k_col, v_col, bias, window, has_sink, q_tiles=2):
    bsz, s, _ = att.shape
    n_groups, rows, span = bias.shape
    width = n_groups * (rows // QB) * HEAD_DIM
    tq = q_tiles * QB
    return pl.pallas_call(
        functools.partial(_band_kernel, window=window, has_sink=has_sink, q_tiles=q_tiles),
        grid=(bsz, s // tq),
        in_specs=[
            pl.BlockSpec((1, tq, width), lambda b, i: (b, i, q_col // width)),
            pl.BlockSpec((1, s, LANES), lambda b, i: (b, 0, k_col // LANES)),
            pl.BlockSpec((1, s, LANES), lambda b, i: (b, 0, v_col // LANES)),
            _resident((n_groups, rows, span), lambda b, i: (0, 0, 0)),
        ],
        out_specs=pl.BlockSpec((1, tq, width), lambda b, i: (b, i, 0)),
        out_shape=jax.ShapeDtypeStruct((bsz, s, width), BF16),
        compiler_params=_cparams(("arbitrary", "arbitrary")),
        name=f"band{window}",
    )(att, att, att, bias)


def _compress_kernel(rk_ref, rv_ref, pos_ref, w1_ref, w2_ref, kc_ref, vc_ref):
    n_rows = rk_ref.shape[2]
    last = lax.broadcasted_iota(jnp.int32, (n_rows, 1), 0) == n_rows - 1
    for which, (r_ref, o_ref) in enumerate(((rk_ref, kc_ref), (rv_ref, vc_ref))):
        halves = []
        for g in range(B_KV):
            r = r_ref[0, g].astype(F32)
            lo = (r + pos_ref[which, 0]).astype(BF16)
            hi = (r + pos_ref[which, 1]).astype(BF16)
            hid = _dot(lo, w1_ref[which, 0]) + pltpu.roll(_dot(hi, w1_ref[which, 1]), n_rows - 1, 0)
            out = _dot(jax.nn.gelu(hid).astype(BF16), w2_ref[which])
            halves.append(jnp.where(last, 0.0, out))
        o_ref[0] = jnp.concatenate(halves, axis=1).astype(BF16)


def _compress(rk, rv, pos, w1, w2):
    bsz, g, n_rows, width = rk.shape
    r_spec = pl.BlockSpec((1, g, n_rows, width), lambda b: (b, 0, 0, 0))
    o_spec = pl.BlockSpec((1, n_rows, g * HEAD_DIM), lambda b: (b, 0, 0))
    o_shape = jax.ShapeDtypeStruct((bsz, n_rows, g * HEAD_DIM), BF16)
    return pl.pallas_call(
        _compress_kernel,
        grid=(bsz,),
        in_specs=[
            r_spec, r_spec,
            pl.BlockSpec(pos.shape, lambda b: (0, 0, 0, 0)),
            pl.BlockSpec(w1.shape, lambda b: (0, 0, 0, 0)),
            pl.BlockSpec(w2.shape, lambda b: (0, 0, 0)),
        ],
        out_specs=[o_spec, o_spec],
        out_shape=[o_shape, o_shape],
        compiler_params=_cparams(("arbitrary",)),
        name="nsa_compress",
    )(rk, rv, pos, w1, w2)


def _cmpsel_kernel(q_ref, kc_ref, vc_ref, ov_ref, o_ref, sel_ref):
    i = pl.program_id(1)
    q = q_ref[0].astype(F32)
    kc = kc_ref[0]
    vc = vc_ref[0]
    n_cmp_rows = kc.shape[0]
    hpg = B_HEADS // B_KV
    rows = hpg * QB
    t_stack = i * QB + lax.broadcasted_iota(jnp.int32, (rows, 1), 0) % QB
    n_idx = lax.broadcasted_iota(jnp.int32, (1, n_cmp_rows), 1)
    cmp_ok = (n_idx * CMP_STRIDE + CMP_BLOCK - 1 <= t_stack) & (n_idx < n_cmp_rows - 1)
    n_sel = ov_ref.shape[0]
    blk = lax.broadcasted_iota(jnp.int32, (n_sel, QB), 0)
    t = i * QB + lax.broadcasted_iota(jnp.int32, (n_sel, QB), 1)
    cur = t // SEL_BLOCK
    forced = (blk == 0) | (blk == cur) | (blk == cur - 1)
    future = blk * SEL_BLOCK > t
    ov = ov_ref[...]
    scores = [_dot_nt(_stack_group_queries(q, g, hpg), kc) for g in range(B_KV)]
    probs = []
    for z in scores:
        logits = jnp.where(cmp_ok, z, NEG)
        m = jnp.max(logits, axis=-1, keepdims=True)
        e = jnp.where(cmp_ok, jnp.exp(logits - m), 0.0)
        ssum = jnp.sum(e, axis=-1, keepdims=True)
        probs.append(e / jnp.where(ssum > 0, ssum, 1.0))
    outs = [_unstack_group_outputs(_dot(p.astype(BF16), vc), g, hpg) for g, p in enumerate(probs)]
    imps = []
    for p in probs:
        p_heads = p[0:QB]
        for hp in range(1, hpg):
            p_heads = p_heads + p[hp * QB:(hp + 1) * QB]
        hi, mid, lo = _split3(p_heads)
        imps.append(_dot_nt(ov, hi) + _dot_nt(ov, mid) + _dot_nt(ov, lo))
    sel_t = []
    for imp in imps:
        prio = jnp.where(forced, jnp.inf, jnp.where(future, -jnp.inf, imp))
        rank = jnp.zeros((n_sel, QB), F32)
        for c in range(n_sel):
            cand = prio[c:c + 1, :]
            rank = rank + jnp.where(blk > c, jnp.where(cand >= prio, 1.0, 0.0), jnp.where(cand > prio, 1.0, 0.0))
        sel_t.append(jnp.where(rank < min(SEL_TOPN, n_sel), 1.0, 0.0))
    sel_t.append(jnp.zeros((LANES - B_KV * n_sel, QB), F32))
    o_ref[0] = jnp.concatenate(outs, axis=1).astype(BF16)
    sel_ref[0] = jnp.concatenate(sel_t, axis=0).T.astype(BF16)


def _cmpsel(att, kc, vc, overlap):
    bsz, s, _ = att.shape
    n_rows = kc.shape[1]
    width = B_HEADS * HEAD_DIM
    return pl.pallas_call(
        _cmpsel_kernel,
        grid=(bsz, s // QB),
        in_specs=[
            pl.BlockSpec((1, QB, width), lambda b, i: (b, i, ATT_COLS["b_q"] // width)),
            pl.BlockSpec((1, n_rows, LANES), lambda b, i: (b, 0, 0)),
            pl.BlockSpec((1, n_rows, LANES), lambda b, i: (b, 0, 0)),
            pl.BlockSpec(overlap.shape, lambda b, i: (0, 0)),
        ],
        out_specs=[pl.BlockSpec((1, QB, width), lambda b, i: (b, i, 0)),
                   pl.BlockSpec((1, QB, LANES), lambda b, i: (b, i, 0))],
        out_shape=[jax.ShapeDtypeStruct((bsz, s, width), BF16), jax.ShapeDtypeStruct((bsz, s, LANES), BF16)],
        compiler_params=_cparams(("arbitrary", "arbitrary")),
        name="nsa_cmpsel",
    )(att, kc, vc, overlap)


def _slc_kernel(q_ref, k_ref, v_ref, sel_ref, bias_ref, far_ref, o_ref, *, n_sel):
    i = pl.program_id(1)
    hpg = B_HEADS // B_KV
    rows = hpg * QB
    q = q_ref[0].astype(F32)
    sel = sel_ref[0]
    sel_row = lax.broadcasted_iota(jnp.int32, (LANES, QB), 0)
    key_blk = lax.broadcasted_iota(jnp.int32, (LANES, QB), 1) // SEL_BLOCK
    iq = lax.broadcasted_iota(jnp.int32, (QB, QB), 0)
    jk = lax.broadcasted_iota(jnp.int32, (QB, QB), 1)
    vhalf = _lane_half((QB, LANES))
    qs = [_stack_group_queries(q, g, hpg) for g in range(B_KV)]

    def tile_step(j, carry, near, causal):
        st = pl.multiple_of(j * QB, QB)
        kt = k_ref[0, pl.ds(st, QB), :]
        vt = v_ref[0, pl.ds(st, QB), :]
        picked = []
        for g in range(B_KV):
            expand = jnp.where(sel_row == g * n_sel + j * (QB // SEL_BLOCK) + key_blk, 1.0, 0.0).astype(BF16)
            picked.append(_dot(sel, expand))
        zs = [_dot_nt(qs[g], kt) for g in range(B_KV)]
        es, m_news = [], []
        for g in range(B_KV):
            ok = picked[g] > 0.5
            if causal is not None:
                ok = ok & causal
            z = zs[g] + (far_ref[g] if near is None else bias_ref[g, :, near * QB:(near + 1) * QB])
            logits = jnp.concatenate(
                [jnp.where(ok, z[hp * QB:(hp + 1) * QB], NEG) for hp in range(hpg)], axis=0)
            m_new = jnp.maximum(carry[g][0], jnp.max(logits, axis=-1, keepdims=True))
            es.append(jnp.exp(logits - m_new).astype(BF16))
            m_news.append(m_new)
        new = []
        for g in range(B_KV):
            m, acc = carry[g]
            pv = _dot(es[g], jnp.where(vhalf == g, vt, jnp.ones_like(vt)))
            new.append((m_news[g], jnp.exp(m - m_news[g]) * acc + pv))
        return tuple(new)

    init = tuple((jnp.full((rows, 1), NEG, F32), jnp.zeros((rows, LANES), F32)) for _ in range(B_KV))
    carry = tile_step(i, init, 1, jk <= iq)
    carry = tile_step(jnp.maximum(i - 1, 0), carry, 0, (i >= 1) & (jk >= 0))
    carry = lax.fori_loop(0, jnp.maximum(i - 1, 0), lambda j, c: tile_step(j, c, None, None), carry)
    outs = [_unstack_group_outputs(acc / pltpu.roll(acc, HEAD_DIM, 1), g, hpg) for g, (_, acc) in enumerate(carry)]
    o_ref[0] = jnp.concatenate(outs, axis=1).astype(BF16)


def _slc_attention(att, sel, bias_near, bias_far, n_sel):
    bsz, s, _ = att.shape
    width = B_HEADS * HEAD_DIM
    n_groups, rows, _ = bias_near.shape
    return pl.pallas_call(
        functools.partial(_slc_kernel, n_sel=n_sel),
        grid=(bsz, s // QB),
        in_specs=[
            pl.BlockSpec((1, QB, width), lambda b, i: (b, i, ATT_COLS["b_q"] // width)),
            pl.BlockSpec((1, s, LANES), lambda b, i: (b, 0, ATT_COLS["b_k_slc"] // LANES)),
            pl.BlockSpec((1, s, LANES), lambda b, i: (b, 0, ATT_COLS["b_v_slc"] // LANES)),
            pl.BlockSpec((1, QB, LANES), lambda b, i: (b, i, 0)),
            _resident((n_groups, rows, 2 * QB), lambda b, i: (0, 0, 0)),
            pl.BlockSpec((n_groups, rows, 1), lambda b, i: (0, 0, 0)),
        ],
        out_specs=pl.BlockSpec((1, QB, width), lambda b, i: (b, i, 0)),
        out_shape=jax.ShapeDtypeStruct((bsz, s, width), BF16),
        compiler_params=_cparams(("arbitrary", "arbitrary")),
        name="nsa_slc",
    )(att, att, att, sel, bias_near, bias_far)


def _stick_kernel(q_ref, k_ref, v_ref, o_ref):
    i = pl.program_id(1)
    pairs = C_HEADS // 2
    rows = 2 * QB
    half = _lane_half((QB, LANES))
    strict = (lax.broadcasted_iota(jnp.int32, (rows, QB), 1)
              < lax.broadcasted_iota(jnp.int32, (rows, QB), 0) % QB)
    later_than = jnp.where(lax.broadcasted_iota(jnp.int32, (QB, QB), 0)
                           > lax.broadcasted_iota(jnp.int32, (QB, QB), 1), 1.0, 0.0).astype(BF16)
    qs = []
    for p in range(pairs):
        q = q_ref[0, :, p * LANES:(p + 1) * LANES].astype(F32) * (1.0 / math.sqrt(HEAD_DIM))
        qs.append(jnp.concatenate([jnp.where(half == 0, q, 0.0), jnp.where(half == 1, q, 0.0)],
                                  axis=0).astype(BF16))

    def tile_step(j, carry, mask):
        st = pl.multiple_of(j * QB, QB)
        zs = [_dot_nt(qs[p], k_ref[0, pl.ds(st, QB), p * LANES:(p + 1) * LANES]) for p in range(pairs)]
        log_beta, log_keep, pieces = [], [], []
        for p in range(pairs):
            z = zs[p]
            lb = jnp.minimum(z, 0.0) - jnp.log(1.0 + jnp.exp(-jnp.abs(z)))
            lk = lb - z
            if mask is not None:
                lk = jnp.where(mask, lk, 0.0)
            hi = lk.astype(BF16)
            log_beta.append(lb)
            log_keep.append(lk)
            pieces.append((hi, (lk - hi.astype(F32)).astype(BF16)))
        later = [_dot(hi, later_than) + _dot(lo, later_than) for hi, lo in pieces]
        ws = []
        for p in range(pairs):
            w = jnp.exp(log_beta[p] + later[p] + carry[p][0])
            if mask is not None:
                w = jnp.where(mask, w, 0.0)
            ws.append(w.astype(BF16))
        new = []
        for p in range(pairs):
            tail, acc = carry[p]
            vt = v_ref[0, pl.ds(st, QB), p * LANES:(p + 1) * LANES]
            new.append((tail + jnp.sum(log_keep[p], axis=-1, keepdims=True), acc + _dot(ws[p], vt)))
        return tuple(new)

    init = tuple((jnp.zeros((rows, 1), F32), jnp.zeros((rows, LANES), F32)) for _ in range(pairs))
    carry = tile_step(i, init, strict)
    carry = lax.fori_loop(0, i, lambda step, c: tile_step(i - 1 - step, c, None), carry)
    o_ref[0] = jnp.concatenate([jnp.where(half == 0, acc[0:QB], acc[QB:rows]) for _, acc in carry],
                               axis=1).astype(BF16)


def _stick_attention(att):
    bsz, s, _ = att.shape
    width = C_HEADS * HEAD_DIM
    return pl.pallas_call(
        _stick_kernel,
        grid=(bsz, s // QB),
        in_specs=[
            pl.BlockSpec((1, QB, width), lambda b, i: (b, i, ATT_COLS["c_q"] // width)),
            pl.BlockSpec((1, s, width), lambda b, i: (b, 0, ATT_COLS["c_k"] // width)),
            pl.BlockSpec((1, s, width), lambda b, i: (b, 0, ATT_COLS["c_v"] // width)),
        ],
        out_specs=pl.BlockSpec((1, QB, width), lambda b, i: (b, i, 0)),
        out_shape=jax.ShapeDtypeStruct((bsz, s, width), BF16),
        compiler_params=_cparams(("arbitrary", "arbitrary")),
        name="stick",
    )(att, att, att)


def _merge_kernel(x_ref, mod_ref, lnpost_ref, ya_ref, oc_ref, os_ref, ow_ref, yc_ref, mg_ref, bg_ref,
                  wb_ref, wo_ref, o_ref):
    d = x_ref.shape[2]
    bw = ya_ref.shape[2]
    head_gate = jax.nn.sigmoid(bg_ref[0])
    gate_row = lax.broadcasted_iota(jnp.int32, (LANES, bw), 0)
    head_of_col = lax.broadcasted_iota(jnp.int32, (LANES, bw), 1) // HEAD_DIM
    y_b = jnp.zeros(ya_ref.shape[1:], F32)
    for br, ref in enumerate((oc_ref, os_ref, ow_ref)):
        expand = jnp.where(gate_row == br * B_HEADS + head_of_col, 1.0, 0.0).astype(BF16)
        y_b = y_b + _dot_exact(head_gate, expand) * ref[0].astype(F32)
    merged = jnp.zeros((x_ref.shape[1], d), F32)
    for br, y in enumerate((ya_ref[0], y_b.astype(BF16), yc_ref[0])):
        merged = merged + jax.nn.sigmoid(mg_ref[0, :, br * d:(br + 1) * d]) * _dot(y, wb_ref[br])
    t = _dot(merged.astype(BF16), wo_ref[...])
    gate = mod_ref[0, 5:6, :]
    o_ref[0] = x_ref[0] + gate * _rms(t, lnpost_ref[...])


def _merge(x, mod, ln_post, y_a, o_cmp, o_slc, o_win, y_c, gates, w_branch, w_out, tm=256):
    bsz, s, d = x.shape
    bw = y_a.shape[2]
    row = lambda b, i: (b, i, 0)
    const2 = lambda b, i: (0, 0)
    branch = pl.BlockSpec((1, tm, bw), row)
    return pl.pallas_call(
        _merge_kernel,
        grid=(bsz, s // tm),
        in_specs=[
            pl.BlockSpec((1, tm, d), row),
            pl.BlockSpec((1, N_SUB * 3, d), lambda b, i: (b, 0, 0)),
            pl.BlockSpec((1, d), const2),
            branch, branch, branch, branch, branch,
            pl.BlockSpec((1, tm, MERGE_W), row),
            pl.BlockSpec((1, tm, LANES), lambda b, i: (b, i, MERGE_W // LANES)),
            _resident(w_branch.shape, lambda b, i: (0, 0, 0)),
            _resident(w_out.shape, const2),
        ],
        out_specs=pl.BlockSpec((1, tm, d), row),
        out_shape=jax.ShapeDtypeStruct(x.shape, F32),
        compiler_params=_cparams(("arbitrary", "arbitrary")),
        name="merge",
    )(x, mod, ln_post, y_a, o_cmp, o_slc, o_win, y_c, gates, gates, w_branch, w_out)


def _t5_bucket(dist):
    max_exact = REL_BUCKETS // 2
    d = jnp.maximum(dist, 0)
    ratio = jnp.log(jnp.maximum(d, 1).astype(F32) / max_exact) / math.log(REL_MAX_DIST / max_exact)
    large = jnp.minimum(max_exact + (ratio * (REL_BUCKETS - max_exact)).astype(jnp.int32), REL_BUCKETS - 1)
    return jnp.where(d < max_exact, d, large)


def _band_bias(table, window, n_groups, mask_band=True):
    span = window + QB
    dist = window + jnp.arange(QB)[:, None] - jnp.arange(span)[None, :]
    bias = jnp.moveaxis(table[_t5_bucket(dist)], -1, 0).astype(F32)
    if mask_band:
        bias = jnp.where(((dist >= 0) & (dist < window))[None], bias, NEG)
    return bias.reshape(n_groups, -1, span)


def _with_sink_column(bias, sinks):
    n_groups, rows, _ = bias.shape
    col = jnp.repeat(sinks.astype(F32), QB).reshape(n_groups, rows)
    return bias.at[:, :, 0].set(col)


def _far_bucket_is_constant():
    d = np.arange(QB + 1, 1 << 16).astype(np.float32)
    max_exact = REL_BUCKETS // 2
    ratio = np.log(d / max_exact) / math.log(REL_MAX_DIST / max_exact)
    return bool(np.all(max_exact + (ratio * (REL_BUCKETS - max_exact)).astype(np.int32) >= REL_BUCKETS - 1))


def _overlap_matrix(n_rows, n_sel):
    cstart = np.arange(n_rows)[None, :] * CMP_STRIDE
    sstart = np.arange(n_sel)[:, None] * SEL_BLOCK
    ov = (cstart < sstart + SEL_BLOCK) & (cstart + CMP_BLOCK > sstart) & (np.arange(n_rows)[None, :] < n_rows - 1)
    return jnp.asarray(ov, dtype=BF16)


def _layer_weights(w_in_l, d):
    src = _src_layout(d)
    order = sorted(ATT_COLS, key=ATT_COLS.get)
    w_att = jnp.concatenate([w_in_l[:, src[n][0]:src[n][1]] for n in order], axis=1).astype(BF16)
    g0, g1 = src["b_gate"]
    m0, m1 = src["merge_gate"]
    pad = jnp.zeros((d, LANES - (g1 - g0)), w_in_l.dtype)
    w_gate = jnp.concatenate([w_in_l[:, m0:m1], w_in_l[:, g0:g1], pad], axis=1).astype(BF16)
    return w_att, w_gate


def _compress_rows(att, name):
    bsz, s, _ = att.shape
    c0 = ATT_COLS[name]
    kv = att[:, :, c0:c0 + B_KV * HEAD_DIM].reshape(bsz, s // CMP_STRIDE, CMP_STRIDE, B_KV, HEAD_DIM)
    return kv.transpose(0, 3, 1, 2, 4).reshape(bsz, B_KV, s // CMP_STRIDE, CMP_STRIDE * HEAD_DIM)


def kernel(x, c, rel_bias, ada_w, ada_b, ln_pre, ln_post, ffn_w_gate, ffn_w_up, ffn_w_down, w_in, attn_sinks,
           cmp_pos, cmp_w1, cmp_w2, w_branch, w_out):
    bsz, s, d = x.shape
    depth = ada_w.shape[0]
    assert s % QB == 0 and s // CMP_STRIDE == LANES and CMP_BLOCK == 2 * CMP_STRIDE
    assert d == MERGE_W // N_BRANCH and A_WINDOW == QB and _far_bucket_is_constant()
    n_rows = s // CMP_STRIDE
    n_sel = s // SEL_BLOCK
    hpg_a, hpg_b = A_HEADS // A_KV, B_HEADS // B_KV

    mod = _modulation(c, ada_w, ada_b)
    table_a = rel_bias[:, :A_HEADS]
    table_b = rel_bias[:, A_HEADS:A_HEADS + B_HEADS]
    bias_a = _band_bias(table_a, A_WINDOW, A_KV)
    bias_w = _band_bias(table_b, B_WINDOW, B_KV)
    bias_near = _band_bias(table_b, QB, B_KV, mask_band=False)
    bias_far = jnp.repeat(table_b[REL_BUCKETS - 1].astype(F32), QB).reshape(B_KV, hpg_b * QB, 1)
    overlap = _overlap_matrix(n_rows, n_sel)
    half_w = CMP_STRIDE * HEAD_DIM

    for l in range(depth):
        mod_l = mod[l]
        x = _ffn(x, mod_l, ln_pre[l, 0:1], ln_post[l, 0:1], ffn_w_gate[l, 0].astype(BF16),
                 ffn_w_up[l, 0].astype(BF16), ffn_w_down[l, 0].astype(BF16), sub=0)

        w_att, w_gate = _layer_weights(w_in[l], d)
        att, gates = _inproj(x, mod_l, ln_pre[l, 1:2], w_att, w_gate)
        bias_a_l = _with_sink_column(bias_a, attn_sinks[l])
        y_a = _band_attention(att, ATT_COLS["a_q"], ATT_COLS["a_k"], ATT_COLS["a_v"], bias_a_l, A_WINDOW, True)
        kc, vc = _compress(
            _compress_rows(att, "b_k_cmp"), _compress_rows(att, "b_v_cmp"),
            cmp_pos[l].reshape(2, 2, 1, half_w).astype(F32),
            cmp_w1[l].reshape(2, 2, half_w, CMP_HIDDEN).astype(BF16), cmp_w2[l].astype(BF16))
        o_cmp, sel = _cmpsel(att, kc, vc, overlap)
        o_slc = _slc_attention(att, sel, bias_near, bias_far, n_sel)
        o_win = _band_attention(att, ATT_COLS["b_q"], ATT_COLS["b_k_win"], ATT_COLS["b_v_win"], bias_w, B_WINDOW,
                                False)
        y_c = _stick_attention(att)
        x = _merge(x, mod_l, ln_post[l, 1:2], y_a, o_cmp, o_slc, o_win, y_c, gates,
                   w_branch[l].astype(BF16), w_out[l].astype(BF16))

        x = _ffn(x, mod_l, ln_pre[l, 2:3], ln_post[l, 2:3], ffn_w_gate[l, 1].astype(BF16),
                 ffn_w_up[l, 1].astype(BF16), ffn_w_down[l, 1].astype(BF16), sub=2)
    return x
```

```python
import functools
import math

import numpy as np
import jax
import jax.numpy as jnp
from jax import lax
from jax.experimental import pallas as pl
from jax.experimental.pallas import tpu as pltpu

F32 = jnp.float32
BF16 = jnp.bfloat16

HEAD_DIM = 64
A_HEADS, A_KV, A_WINDOW = 8, 2, 128
B_HEADS, B_KV, B_WINDOW = 8, 2, 512
CMP_BLOCK, CMP_STRIDE, CMP_HIDDEN = 32, 16, 128
SEL_BLOCK, SEL_TOPN = 64, 16
C_HEADS = 8
BRANCH_WIDTH = 512
N_BRANCH = 3
REL_BUCKETS, REL_MAX_DIST = 32, 128
N_SUB = 3
EPS = 1e-6
FFN_RES = 0.5
NEG = -1e30

LANES = 128
QB = 128
SLC_TILE = 256
STICK_TILE = 256
VMEM_LIMIT = 56 * 1024 * 1024

ATT_COLS = {
    "a_q": 0, "b_q": 512, "c_q": 1024, "c_k": 1536, "c_v": 2048,
    "a_k": 2560, "a_v": 2688, "b_k_cmp": 2816, "b_v_cmp": 2944,
    "b_k_slc": 3072, "b_v_slc": 3200, "b_k_win": 3328, "b_v_win": 3456,
}
ATT_W = 3584
MERGE_W = 3072
GATE_W = MERGE_W + LANES


def _src_layout(d_model):
    widths = [
        ("a_q", A_HEADS * HEAD_DIM), ("a_k", A_KV * HEAD_DIM), ("a_v", A_KV * HEAD_DIM),
        ("b_q", B_HEADS * HEAD_DIM),
        ("b_k_cmp", B_KV * HEAD_DIM), ("b_v_cmp", B_KV * HEAD_DIM),
        ("b_k_slc", B_KV * HEAD_DIM), ("b_v_slc", B_KV * HEAD_DIM),
        ("b_k_win", B_KV * HEAD_DIM), ("b_v_win", B_KV * HEAD_DIM),
        ("b_gate", 3 * B_HEADS),
        ("c_q", C_HEADS * HEAD_DIM), ("c_k", C_HEADS * HEAD_DIM), ("c_v", C_HEADS * HEAD_DIM),
        ("merge_gate", N_BRANCH * d_model),
    ]
    out, off = {}, 0
    for name, w in widths:
        out[name] = (off, off + w)
        off += w
    return out


def _cparams(sem):
    return pltpu.CompilerParams(dimension_semantics=sem, vmem_limit_bytes=VMEM_LIMIT)


def _resident(shape, index_map):
    return pl.BlockSpec(shape, index_map, pipeline_mode=pl.Buffered(1))


def _rms(x, gain):
    return x * lax.rsqrt(jnp.mean(x * x, axis=-1, keepdims=True) + EPS) * gain


def _modulated(x, mod_ref, sub, gain):
    shift = mod_ref[0, 3 * sub:3 * sub + 1, :]
    scale = mod_ref[0, 3 * sub + 1:3 * sub + 2, :]
    return _rms(x, gain) * (1.0 + scale) + shift


def _dot(a, b):
    return jnp.dot(a, b, preferred_element_type=F32)


def _dot_nt(a, b):
    return lax.dot_general(a, b, (((1,), (1,)), ((), ())), preferred_element_type=F32)


def _split3(x):
    hi = x.astype(BF16)
    r1 = x - hi.astype(F32)
    mid = r1.astype(BF16)
    lo = (r1 - mid.astype(F32)).astype(BF16)
    return hi, mid, lo


def _dot_exact(x, sel):
    hi, mid, lo = _split3(x)
    return _dot(hi, sel) + _dot(mid, sel) + _dot(lo, sel)


def _lane_half(shape):
    return (lax.broadcasted_iota(jnp.int32, shape, 1) % LANES) // HEAD_DIM


def _stack_group_queries(q, group, heads_per_group):
    half = _lane_half((q.shape[0], LANES))
    parts = []
    for hp in range(heads_per_group):
        h = group * heads_per_group + hp
        tile = q[:, (h // 2) * LANES:(h // 2 + 1) * LANES]
        if h % 2 != group:
            tile = pltpu.roll(tile, HEAD_DIM, 1)
        parts.append(jnp.where(half == group, tile, 0.0))
    scale = 1.0 / math.sqrt(HEAD_DIM)
    return (jnp.concatenate(parts, axis=0) * scale).astype(BF16)


def _unstack_group_outputs(pv, group, heads_per_group):
    tile = pv.shape[0] // heads_per_group
    half = _lane_half((tile, LANES))
    tiles = []
    for pair in range(heads_per_group // 2):
        even = pv[(2 * pair) * tile:(2 * pair + 1) * tile]
        odd = pv[(2 * pair + 1) * tile:(2 * pair + 2) * tile]
        if group != 0:
            even = pltpu.roll(even, HEAD_DIM, 1)
        if group != 1:
            odd = pltpu.roll(odd, HEAD_DIM, 1)
        tiles.append(jnp.where(half == 0, even, odd))
    return jnp.concatenate(tiles, axis=1)


def _mod_kernel(c_ref, w_ref, b_ref, o_ref):
    c = c_ref[...]
    act = (c * jax.nn.sigmoid(c)).astype(BF16)
    o_ref[0] = _dot(act, w_ref[0].astype(BF16)) + b_ref[0]


def _modulation(c, ada_w, ada_b):
    depth, d, n = ada_w.shape
    bsz = c.shape[0]
    tn = n // 4
    out = pl.pallas_call(
        _mod_kernel,
        grid=(depth, n // tn),
        in_specs=[
            pl.BlockSpec((bsz, d), lambda l, j: (0, 0)),
            pl.BlockSpec((1, d, tn), lambda l, j: (l, 0, j)),
            pl.BlockSpec((1, 1, tn), lambda l, j: (l, 0, j)),
        ],
        out_specs=pl.BlockSpec((1, bsz, tn), lambda l, j: (l, 0, j)),
        out_shape=jax.ShapeDtypeStruct((depth, bsz, n), F32),
        compiler_params=_cparams(("arbitrary", "arbitrary")),
        name="adaln_mod",
    )(c, ada_w, ada_b.reshape(depth, 1, n))
    return out.reshape(depth, bsz, N_SUB * 3, d)


def _ffn_kernel(x_ref, mod_ref, lnpre_ref, lnpost_ref, wg_ref, wu_ref, wd_ref, o_ref, acc_ref, *, sub, tf):
    x = x_ref[0]
    h = _modulated(x, mod_ref, sub, lnpre_ref[...]).astype(BF16)
    d_ff = wg_ref.shape[1]
    for j in range(d_ff // tf):
        g = _dot(h, wg_ref[:, j * tf:(j + 1) * tf])
        u = _dot(h, wu_ref[:, j * tf:(j + 1) * tf])
        a = (g * jax.nn.sigmoid(g) * u).astype(BF16)
        part = _dot(a, wd_ref[j * tf:(j + 1) * tf, :])
        if j == 0:
            acc_ref[...] = part
        else:
            acc_ref[...] += part
    gate = mod_ref[0, 3 * sub + 2:3 * sub + 3, :]
    o_ref[0] = x + FFN_RES * gate * _rms(acc_ref[...], lnpost_ref[...])


def _ffn(x, mod, ln_pre, ln_post, wg, wu, wd, sub, tm=512, tf=256):
    bsz, s, d = x.shape
    d_ff = wg.shape[1]
    row = lambda b, i: (b, i, 0)
    const2 = lambda b, i: (0, 0)
    return pl.pallas_call(
        functools.partial(_ffn_kernel, sub=sub, tf=tf),
        grid=(bsz, s // tm),
        in_specs=[
            pl.BlockSpec((1, tm, d), row),
            pl.BlockSpec((1, N_SUB * 3, d), lambda b, i: (b, 0, 0)),
            pl.BlockSpec((1, d), const2),
            pl.BlockSpec((1, d), const2),
            _resident((d, d_ff), const2),
            _resident((d, d_ff), const2),
            _resident((d_ff, d), const2),
        ],
        out_specs=pl.BlockSpec((1, tm, d), row),
        out_shape=jax.ShapeDtypeStruct(x.shape, F32),
        scratch_shapes=[pltpu.VMEM((tm, d), F32)],
        compiler_params=_cparams(("arbitrary", "arbitrary")),
        name="ffn",
    )(x, mod, ln_pre, ln_post, wg, wu, wd)


def _inproj_kernel(x_ref, mod_ref, lnpre_ref, watt_ref, wgate_ref, att_ref, gate_ref, *, att_chunk, gate_chunk):
    h = _modulated(x_ref[0], mod_ref, 1, lnpre_ref[...]).astype(BF16)
    for c0 in range(0, ATT_W, att_chunk):
        att_ref[0, :, c0:c0 + att_chunk] = _dot(h, watt_ref[:, c0:c0 + att_chunk]).astype(BF16)
    for c0 in range(0, GATE_W, gate_chunk):
        gate_ref[0, :, c0:c0 + gate_chunk] = _dot(h, wgate_ref[:, c0:c0 + gate_chunk])


def _inproj(x, mod, ln_pre, w_att, w_gate, tm=256):
    bsz, s, d = x.shape
    row = lambda b, i: (b, i, 0)
    const2 = lambda b, i: (0, 0)
    return pl.pallas_call(
        functools.partial(_inproj_kernel, att_chunk=512, gate_chunk=640),
        grid=(bsz, s // tm),
        in_specs=[
            pl.BlockSpec((1, tm, d), row),
            pl.BlockSpec((1, N_SUB * 3, d), lambda b, i: (b, 0, 0)),
            pl.BlockSpec((1, d), const2),
            _resident((d, ATT_W), const2),
            _resident((d, GATE_W), const2),
        ],
        out_specs=[pl.BlockSpec((1, tm, ATT_W), row), pl.BlockSpec((1, tm, GATE_W), row)],
        out_shape=[jax.ShapeDtypeStruct((bsz, s, ATT_W), BF16), jax.ShapeDtypeStruct((bsz, s, GATE_W), F32)],
        compiler_params=_cparams(("arbitrary", "arbitrary")),
        name="inproj",
    )(x, mod, ln_pre, w_att, w_gate)


def _band_kernel(q_ref, k_ref, v_ref, bias_ref, o_ref, *, window, has_sink, q_tiles):
    nkb = window // QB + 1
    span = nkb * QB
    n_groups = bias_ref.shape[0]
    hpg = bias_ref.shape[1] // QB
    jk = lax.broadcasted_iota(jnp.int32, (1, span), 1)
    row0 = lax.broadcasted_iota(jnp.int32, (span, LANES), 0) == 0
    vhalf = _lane_half((span, LANES))
    chains = [(u, g) for u in range(q_tiles) for g in range(n_groups)]
    vspans, in_seqs, scores = [], [], []
    for u in range(q_tiles):
        i = pl.program_id(1) * q_tiles + u
        starts = [jnp.maximum(i - (nkb - 1) + kb, 0) * QB for kb in range(nkb)]
        kspan = jnp.concatenate([k_ref[0, pl.ds(pl.multiple_of(st, QB), QB), :] for st in starts], axis=0)
        vspan = jnp.concatenate([v_ref[0, pl.ds(pl.multiple_of(st, QB), QB), :] for st in starts], axis=0)
        in_seq = jk >= (nkb - 1 - i) * QB
        if has_sink:
            kspan = jnp.where(row0, jnp.zeros_like(kspan), kspan)
            vspan = jnp.where(row0, jnp.zeros_like(vspan), vspan)
            in_seq = in_seq | (jk == 0)
        vspans.append(vspan)
        in_seqs.append(in_seq)
        q = q_ref[0, u * QB:(u + 1) * QB, :].astype(F32)
        for g in range(n_groups):
            scores.append(_dot_nt(_stack_group_queries(q, g, hpg), kspan))
    weights = []
    for (u, g), z in zip(chains, scores):
        logits = jnp.where(in_seqs[u], z + bias_ref[g], NEG)
        weights.append(jnp.exp(logits - jnp.max(logits, axis=-1, keepdims=True)).astype(BF16))
    pvs = [_dot(e, jnp.where(vhalf == g, vspans[u], jnp.ones_like(vspans[u]))) for (u, g), e in zip(chains, weights)]
    for u in range(q_tiles):
        outs = []
        for g in range(n_groups):
            pv = pvs[u * n_groups + g]
            outs.append(_unstack_group_outputs(pv / pltpu.roll(pv, HEAD_DIM, 1), g, hpg))
        o_ref[0, u * QB:(u + 1) * QB, :] = jnp.concatenate(outs, axis=1).astype(BF16)


def _band_attention(att, q_col, k_col, v_col, bias, window, has_sink, q_tiles=2):
    bsz, s, _ = att.shape
    n_groups, rows, span = bias.shape
    width = n_groups * (rows // QB) * HEAD_DIM
    tq = q_tiles * QB
    return pl.pallas_call(
        functools.partial(_band_kernel, window=window, has_sink=has_sink, q_tiles=q_tiles),
        grid=(bsz, s // tq),
        in_specs=[
            pl.BlockSpec((1, tq, width), lambda b, i: (b, i, q_col // width)),
            pl.BlockSpec((1, s, LANES), lambda b, i: (b, 0, k_col // LANES)),
            pl.BlockSpec((1, s, LANES), lambda b, i: (b, 0, v_col // LANES)),
            _resident((n_groups, rows, span), lambda b, i: (0, 0, 0)),
        ],
        out_specs=pl.BlockSpec((1, tq, width), lambda b, i: (b, i, 0)),
        out_shape=jax.ShapeDtypeStruct((bsz, s, width), BF16),
        compiler_params=_cparams(("arbitrary", "arbitrary")),
        name=f"band{window}",
    )(att, att, att, bias)


def _compress_kernel(rk_ref, rv_ref, pos_ref, w1_ref, w2_ref, kc_ref, vc_ref):
    n_rows = rk_ref.shape[1]
    last = lax.broadcasted_iota(jnp.int32, (n_rows, 1), 0) == n_rows - 1
    for which, (r_ref, o_ref) in enumerate(((rk_ref, kc_ref), (rv_ref, vc_ref))):
        r = r_ref[0].astype(F32)
        lo = (r + pos_ref[which, 0]).astype(BF16)
        hi = (r + pos_ref[which, 1]).astype(BF16)
        halves = []
        for g in range(B_KV):
            hid = _dot(lo, w1_ref[which, 0, g]) + pltpu.roll(_dot(hi, w1_ref[which, 1, g]), n_rows - 1, 0)
            out = _dot(jax.nn.gelu(hid).astype(BF16), w2_ref[which])
            halves.append(jnp.where(last, 0.0, out))
        o_ref[0] = jnp.concatenate(halves, axis=1).astype(BF16)


def _compress(rk, rv, pos, w1, w2):
    bsz, n_rows, width = rk.shape
    g = B_KV
    r_spec = pl.BlockSpec((1, n_rows, width), lambda b: (b, 0, 0))
    o_spec = pl.BlockSpec((1, n_rows, g * HEAD_DIM), lambda b: (b, 0, 0))
    o_shape = jax.ShapeDtypeStruct((bsz, n_rows, g * HEAD_DIM), BF16)
    return pl.pallas_call(
        _compress_kernel,
        grid=(bsz,),
        in_specs=[
            r_spec, r_spec,
            pl.BlockSpec(pos.shape, lambda b: (0, 0, 0, 0)),
            pl.BlockSpec(w1.shape, lambda b: (0, 0, 0, 0, 0)),
            pl.BlockSpec(w2.shape, lambda b: (0, 0, 0)),
        ],
        out_specs=[o_spec, o_spec],
        out_shape=[o_shape, o_shape],
        compiler_params=_cparams(("arbitrary",)),
        name="nsa_compress",
    )(rk, rv, pos, w1, w2)


def _cmpsel_kernel(q_ref, kc_ref, vc_ref, ov_ref, o_ref, sel_ref):
    i = pl.program_id(1)
    q = q_ref[0].astype(F32)
    kc = kc_ref[0]
    vc = vc_ref[0]
    n_cmp_rows = kc.shape[0]
    hpg = B_HEADS // B_KV
    rows = hpg * QB
    t_stack = i * QB + lax.broadcasted_iota(jnp.int32, (rows, 1), 0) % QB
    n_idx = lax.broadcasted_iota(jnp.int32, (1, n_cmp_rows), 1)
    cmp_ok = (n_idx * CMP_STRIDE + CMP_BLOCK - 1 <= t_stack) & (n_idx < n_cmp_rows - 1)
    n_sel = ov_ref.shape[0]
    blk = lax.broadcasted_iota(jnp.int32, (n_sel, QB), 0)
    t = i * QB + lax.broadcasted_iota(jnp.int32, (n_sel, QB), 1)
    cur = t // SEL_BLOCK
    forced = (blk == 0) | (blk == cur) | (blk == cur - 1)
    future = blk * SEL_BLOCK > t
    ov = ov_ref[...]
    scores = [_dot_nt(_stack_group_queries(q, g, hpg), kc) for g in range(B_KV)]
    probs = []
    for z in scores:
        logits = jnp.where(cmp_ok, z, NEG)
        m = jnp.max(logits, axis=-1, keepdims=True)
        e = jnp.where(cmp_ok, jnp.exp(logits - m), 0.0)
        ssum = jnp.sum(e, axis=-1, keepdims=True)
        probs.append(e / jnp.where(ssum > 0, ssum, 1.0))
    outs = [_unstack_group_outputs(_dot(p.astype(BF16), vc), g, hpg) for g, p in enumerate(probs)]
    imps = []
    for p in probs:
        p_heads = p[0:QB]
        for hp in range(1, hpg):
            p_heads = p_heads + p[hp * QB:(hp + 1) * QB]
        hi, mid, lo = _split3(p_heads)
        imps.append(_dot_nt(ov, hi) + _dot_nt(ov, mid) + _dot_nt(ov, lo))
    sel_t = []
    for imp in imps:
        prio = jnp.where(forced, jnp.inf, jnp.where(future, -jnp.inf, imp))
        rank = jnp.zeros((n_sel, QB), F32)
        for c in range(n_sel):
            cand = prio[c:c + 1, :]
            rank = rank + jnp.where(blk > c, jnp.where(cand >= prio, 1.0, 0.0), jnp.where(cand > prio, 1.0, 0.0))
        sel_t.append(jnp.where(rank < min(SEL_TOPN, n_sel), 1.0, 0.0))
    sel_t.append(jnp.zeros((LANES - B_KV * n_sel, QB), F32))
    o_ref[0] = jnp.concatenate(outs, axis=1).astype(BF16)
    sel_ref[0] = jnp.concatenate(sel_t, axis=0).T.astype(BF16)


def _cmpsel(att, kc, vc, overlap):
    bsz, s, _ = att.shape
    n_rows = kc.shape[1]
    width = B_HEADS * HEAD_DIM
    return pl.pallas_call(
        _cmpsel_kernel,
        grid=(bsz, s // QB),
        in_specs=[
            pl.BlockSpec((1, QB, width), lambda b, i: (b, i, ATT_COLS["b_q"] // width)),
            pl.BlockSpec((1, n_rows, LANES), lambda b, i: (b, 0, 0)),
            pl.BlockSpec((1, n_rows, LANES), lambda b, i: (b, 0, 0)),
            pl.BlockSpec(overlap.shape, lambda b, i: (0, 0)),
        ],
        out_specs=[pl.BlockSpec((1, QB, width), lambda b, i: (b, i, 0)),
                   pl.BlockSpec((1, QB, LANES), lambda b, i: (b, i, 0))],
        out_shape=[jax.ShapeDtypeStruct((bsz, s, width), BF16), jax.ShapeDtypeStruct((bsz, s, LANES), BF16)],
        compiler_params=_cparams(("arbitrary", "arbitrary")),
        name="nsa_cmpsel",
    )(att, kc, vc, overlap)


def _slc_kernel(q_ref, k_ref, v_ref, sel_ref, bias_ref, far_ref, o_ref, *, n_sel):
    i = pl.program_id(1)
    hpg = B_HEADS // B_KV
    tile = sel_ref.shape[1]
    rows = hpg * tile
    q = q_ref[0].astype(F32)
    sel = sel_ref[0]
    sel_row = lax.broadcasted_iota(jnp.int32, (LANES, tile), 0)
    key_blk = lax.broadcasted_iota(jnp.int32, (LANES, tile), 1) // SEL_BLOCK
    iq = lax.broadcasted_iota(jnp.int32, (tile, tile), 0)
    jk = lax.broadcasted_iota(jnp.int32, (tile, tile), 1)
    vhalf = _lane_half((tile, LANES))
    qs = [_stack_group_queries(q, g, hpg) for g in range(B_KV)]

    def tile_step(j, carry, near, causal):
        st = pl.multiple_of(j * tile, tile)
        kt = k_ref[0, pl.ds(st, tile), :]
        vt = v_ref[0, pl.ds(st, tile), :]
        picked = []
        for g in range(B_KV):
            expand = jnp.where(sel_row == g * n_sel + j * (tile // SEL_BLOCK) + key_blk, 1.0, 0.0).astype(BF16)
            picked.append(_dot(sel, expand))
        zs = [_dot_nt(qs[g], kt) for g in range(B_KV)]
        es, m_news = [], []
        for g in range(B_KV):
            ok = picked[g] > 0.5
            if causal is not None:
                ok = ok & causal
            z = zs[g] + (far_ref[g] if near is None else bias_ref[g, :, near * tile:(near + 1) * tile])
            logits = jnp.concatenate(
                [jnp.where(ok, z[hp * tile:(hp + 1) * tile], NEG) for hp in range(hpg)], axis=0)
            m_new = jnp.maximum(carry[g][0], jnp.max(logits, axis=-1, keepdims=True))
            es.append(jnp.exp(logits - m_new).astype(BF16))
            m_news.append(m_new)
        new = []
        for g in range(B_KV):
            m, acc = carry[g]
            pv = _dot(es[g], jnp.where(vhalf == g, vt, jnp.ones_like(vt)))
            new.append((m_news[g], jnp.exp(m - m_news[g]) * acc + pv))
        return tuple(new)

    init = tuple((jnp.full((rows, 1), NEG, F32), jnp.zeros((rows, LANES), F32)) for _ in range(B_KV))
    carry = tile_step(i, init, 1, jk <= iq)
    carry = tile_step(jnp.maximum(i - 1, 0), carry, 0, (i >= 1) & (jk >= 0))
    carry = lax.fori_loop(0, jnp.maximum(i - 1, 0), lambda j, c: tile_step(j, c, None, None), carry)
    outs = [_unstack_group_outputs(acc / pltpu.roll(acc, HEAD_DIM, 1), g, hpg) for g, (_, acc) in enumerate(carry)]
    o_ref[0] = jnp.concatenate(outs, axis=1).astype(BF16)


def _slc_attention(att, sel, bias_near, bias_far, n_sel):
    bsz, s, _ = att.shape
    width = B_HEADS * HEAD_DIM
    n_groups, rows, span = bias_near.shape
    tile = span // 2
    return pl.pallas_call(
        functools.partial(_slc_kernel, n_sel=n_sel),
        grid=(bsz, s // tile),
        in_specs=[
            pl.BlockSpec((1, tile, width), lambda b, i: (b, i, ATT_COLS["b_q"] // width)),
            pl.BlockSpec((1, s, LANES), lambda b, i: (b, 0, ATT_COLS["b_k_slc"] // LANES)),
            pl.BlockSpec((1, s, LANES), lambda b, i: (b, 0, ATT_COLS["b_v_slc"] // LANES)),
            pl.BlockSpec((1, tile, LANES), lambda b, i: (b, i, 0)),
            _resident((n_groups, rows, span), lambda b, i: (0, 0, 0)),
            pl.BlockSpec((n_groups, rows, 1), lambda b, i: (0, 0, 0)),
        ],
        out_specs=pl.BlockSpec((1, tile, width), lambda b, i: (b, i, 0)),
        out_shape=jax.ShapeDtypeStruct((bsz, s, width), BF16),
        compiler_params=_cparams(("arbitrary", "arbitrary")),
        name="nsa_slc",
    )(att, att, att, sel, bias_near, bias_far)


def _stick_kernel(q_ref, k_ref, v_ref, o_ref):
    i = pl.program_id(1)
    pairs = C_HEADS // 2
    tile = q_ref.shape[1]
    rows = 2 * tile
    half = _lane_half((tile, LANES))
    strict = (lax.broadcasted_iota(jnp.int32, (rows, tile), 1)
              < lax.broadcasted_iota(jnp.int32, (rows, tile), 0) % tile)
    later_than = jnp.where(lax.broadcasted_iota(jnp.int32, (2 * tile, tile), 0) % tile
                           > lax.broadcasted_iota(jnp.int32, (2 * tile, tile), 1), 1.0, 0.0).astype(BF16)
    qs = []
    for p in range(pairs):
        q = q_ref[0, :, p * LANES:(p + 1) * LANES].astype(F32) * (1.0 / math.sqrt(HEAD_DIM))
        qs.append(jnp.concatenate([jnp.where(half == 0, q, 0.0), jnp.where(half == 1, q, 0.0)],
                                  axis=0).astype(BF16))

    def tile_step(j, carry, mask):
        st = pl.multiple_of(j * tile, tile)
        zs = [_dot_nt(qs[p], k_ref[0, pl.ds(st, tile), p * LANES:(p + 1) * LANES]) for p in range(pairs)]
        log_beta, log_keep, pieces = [], [], []
        for p in range(pairs):
            z = zs[p]
            lb = jnp.minimum(z, 0.0) - jnp.log(1.0 + jnp.exp(-jnp.abs(z)))
            lk = lb - z
            if mask is not None:
                lk = jnp.where(mask, lk, 0.0)
            hi = lk.astype(BF16)
            log_beta.append(lb)
            log_keep.append(lk)
            pieces.append(jnp.concatenate([hi, (lk - hi.astype(F32)).astype(BF16)], axis=1))
        later = [_dot(piece, later_than) for piece in pieces]
        ws = []
        for p in range(pairs):
            w = jnp.exp(log_beta[p] + later[p] + carry[p][0])
            if mask is not None:
                w = jnp.where(mask, w, 0.0)
            ws.append(w.astype(BF16))
        new = []
        for p in range(pairs):
            tail, acc = carry[p]
            vt = v_ref[0, pl.ds(st, tile), p * LANES:(p + 1) * LANES]
            new.append((tail + jnp.sum(log_keep[p], axis=-1, keepdims=True), acc + _dot(ws[p], vt)))
        return tuple(new)

    init = tuple((jnp.zeros((rows, 1), F32), jnp.zeros((rows, LANES), F32)) for _ in range(pairs))
    carry = tile_step(i, init, strict)
    carry = lax.fori_loop(0, i, lambda step, c: tile_step(i - 1 - step, c, None), carry)
    o_ref[0] = jnp.concatenate([jnp.where(half == 0, acc[0:tile], acc[tile:rows]) for _, acc in carry],
                               axis=1).astype(BF16)


def _stick_attention(att, tile=STICK_TILE):
    bsz, s, _ = att.shape
    width = C_HEADS * HEAD_DIM
    return pl.pallas_call(
        _stick_kernel,
        grid=(bsz, s // tile),
        in_specs=[
            pl.BlockSpec((1, tile, width), lambda b, i: (b, i, ATT_COLS["c_q"] // width)),
            pl.BlockSpec((1, s, width), lambda b, i: (b, 0, ATT_COLS["c_k"] // width)),
            pl.BlockSpec((1, s, width), lambda b, i: (b, 0, ATT_COLS["c_v"] // width)),
        ],
        out_specs=pl.BlockSpec((1, tile, width), lambda b, i: (b, i, 0)),
        out_shape=jax.ShapeDtypeStruct((bsz, s, width), BF16),
        compiler_params=_cparams(("arbitrary", "arbitrary")),
        name="stick",
    )(att, att, att)


def _merge_kernel(x_ref, mod_ref, lnpost_ref, ya_ref, oc_ref, os_ref, ow_ref, yc_ref, mg_ref, bg_ref,
                  wb_ref, wo_ref, o_ref):
    d = x_ref.shape[2]
    bw = ya_ref.shape[2]
    head_gate = jax.nn.sigmoid(bg_ref[0])
    gate_row = lax.broadcasted_iota(jnp.int32, (LANES, bw), 0)
    head_of_col = lax.broadcasted_iota(jnp.int32, (LANES, bw), 1) // HEAD_DIM
    y_b = jnp.zeros(ya_ref.shape[1:], F32)
    for br, ref in enumerate((oc_ref, os_ref, ow_ref)):
        expand = jnp.where(gate_row == br * B_HEADS + head_of_col, 1.0, 0.0).astype(BF16)
        y_b = y_b + _dot_exact(head_gate, expand) * ref[0].astype(F32)
    merged = jnp.zeros((x_ref.shape[1], d), F32)
    for br, y in enumerate((ya_ref[0], y_b.astype(BF16), yc_ref[0])):
        merged = merged + jax.nn.sigmoid(mg_ref[0, :, br * d:(br + 1) * d]) * _dot(y, wb_ref[br])
    t = _dot(merged.astype(BF16), wo_ref[...])
    gate = mod_ref[0, 5:6, :]
    o_ref[0] = x_ref[0] + gate * _rms(t, lnpost_ref[...])


def _merge(x, mod, ln_post, y_a, o_cmp, o_slc, o_win, y_c, gates, w_branch, w_out, tm=256):
    bsz, s, d = x.shape
    bw = y_a.shape[2]
    row = lambda b, i: (b, i, 0)
    const2 = lambda b, i: (0, 0)
    branch = pl.BlockSpec((1, tm, bw), row)
    return pl.pallas_call(
        _merge_kernel,
        grid=(bsz, s // tm),
        in_specs=[
            pl.BlockSpec((1, tm, d), row),
            pl.BlockSpec((1, N_SUB * 3, d), lambda b, i: (b, 0, 0)),
            pl.BlockSpec((1, d), const2),
            branch, branch, branch, branch, branch,
            pl.BlockSpec((1, tm, MERGE_W), row),
            pl.BlockSpec((1, tm, LANES), lambda b, i: (b, i, MERGE_W // LANES)),
            _resident(w_branch.shape, lambda b, i: (0, 0, 0)),
            _resident(w_out.shape, const2),
        ],
        out_specs=pl.BlockSpec((1, tm, d), row),
        out_shape=jax.ShapeDtypeStruct(x.shape, F32),
        compiler_params=_cparams(("arbitrary", "arbitrary")),
        name="merge",
    )(x, mod, ln_post, y_a, o_cmp, o_slc, o_win, y_c, gates, gates, w_branch, w_out)


def _t5_bucket(dist):
    max_exact = REL_BUCKETS // 2
    d = jnp.maximum(dist, 0)
    ratio = jnp.log(jnp.maximum(d, 1).astype(F32) / max_exact) / math.log(REL_MAX_DIST / max_exact)
    large = jnp.minimum(max_exact + (ratio * (REL_BUCKETS - max_exact)).astype(jnp.int32), REL_BUCKETS - 1)
    return jnp.where(d < max_exact, d, large)


def _band_bias(table, window, n_groups, mask_band=True, tile=QB):
    span = window + tile
    dist = window + jnp.arange(tile)[:, None] - jnp.arange(span)[None, :]
    bias = jnp.moveaxis(table[_t5_bucket(dist)], -1, 0).astype(F32)
    if mask_band:
        bias = jnp.where(((dist >= 0) & (dist < window))[None], bias, NEG)
    return bias.reshape(n_groups, -1, span)


def _with_sink_column(bias, sinks):
    n_groups, rows, _ = bias.shape
    col = jnp.repeat(sinks.astype(F32), QB).reshape(n_groups, rows)
    return bias.at[:, :, 0].set(col)


def _far_bucket_is_constant():
    d = np.arange(QB + 1, 1 << 16).astype(np.float32)
    max_exact = REL_BUCKETS // 2
    ratio = np.log(d / max_exact) / math.log(REL_MAX_DIST / max_exact)
    return bool(np.all(max_exact + (ratio * (REL_BUCKETS - max_exact)).astype(np.int32) >= REL_BUCKETS - 1))


def _overlap_matrix(n_rows, n_sel):
    cstart = np.arange(n_rows)[None, :] * CMP_STRIDE
    sstart = np.arange(n_sel)[:, None] * SEL_BLOCK
    ov = (cstart < sstart + SEL_BLOCK) & (cstart + CMP_BLOCK > sstart) & (np.arange(n_rows)[None, :] < n_rows - 1)
    return jnp.asarray(ov, dtype=BF16)


def _layer_weights(w_in_l, d):
    src = _src_layout(d)
    order = sorted(ATT_COLS, key=ATT_COLS.get)
    w_att = jnp.concatenate([w_in_l[:, src[n][0]:src[n][1]] for n in order], axis=1).astype(BF16)
    g0, g1 = src["b_gate"]
    m0, m1 = src["merge_gate"]
    pad = jnp.zeros((d, LANES - (g1 - g0)), w_in_l.dtype)
    w_gate = jnp.concatenate([w_in_l[:, m0:m1], w_in_l[:, g0:g1], pad], axis=1).astype(BF16)
    return w_att, w_gate


def _compress_rows(att, name):
    bsz, s, _ = att.shape
    c0 = ATT_COLS[name]
    return att[:, :, c0:c0 + LANES].reshape(bsz, s // CMP_STRIDE, CMP_STRIDE * LANES)


def _compress_params(pos, w1):
    hidden = w1.shape[-1]
    pos_rows = jnp.broadcast_to(pos.reshape(2, 2, CMP_STRIDE, 1, HEAD_DIM).astype(F32),
                                (2, 2, CMP_STRIDE, B_KV, HEAD_DIM)).reshape(2, 2, 1, CMP_STRIDE * LANES)
    w1r = w1.reshape(2, 2, CMP_STRIDE, HEAD_DIM, hidden)
    w1g = jnp.einsum('whjdc,gk->whgjkdc', w1r, jnp.eye(B_KV, dtype=w1.dtype))
    return pos_rows, w1g.reshape(2, 2, B_KV, CMP_STRIDE * LANES, hidden).astype(BF16)


def kernel(x, c, rel_bias, ada_w, ada_b, ln_pre, ln_post, ffn_w_gate, ffn_w_up, ffn_w_down, w_in, attn_sinks,
           cmp_pos, cmp_w1, cmp_w2, w_branch, w_out):
    bsz, s, d = x.shape
    depth = ada_w.shape[0]
    assert s % QB == 0 and s // CMP_STRIDE == LANES and CMP_BLOCK == 2 * CMP_STRIDE
    assert d == MERGE_W // N_BRANCH and A_WINDOW == QB and _far_bucket_is_constant()
    n_rows = s // CMP_STRIDE
    n_sel = s // SEL_BLOCK
    hpg_a, hpg_b = A_HEADS // A_KV, B_HEADS // B_KV

    mod = _modulation(c, ada_w, ada_b)
    table_a = rel_bias[:, :A_HEADS]
    table_b = rel_bias[:, A_HEADS:A_HEADS + B_HEADS]
    bias_a = _band_bias(table_a, A_WINDOW, A_KV)
    bias_w = _band_bias(table_b, B_WINDOW, B_KV)
    bias_near = _band_bias(table_b, SLC_TILE, B_KV, mask_band=False, tile=SLC_TILE)
    bias_far = jnp.repeat(table_b[REL_BUCKETS - 1].astype(F32), SLC_TILE).reshape(B_KV, hpg_b * SLC_TILE, 1)
    overlap = _overlap_matrix(n_rows, n_sel)

    for l in range(depth):
        mod_l = mod[l]
        x = _ffn(x, mod_l, ln_pre[l, 0:1], ln_post[l, 0:1], ffn_w_gate[l, 0].astype(BF16),
                 ffn_w_up[l, 0].astype(BF16), ffn_w_down[l, 0].astype(BF16), sub=0)

        w_att, w_gate = _layer_weights(w_in[l], d)
        att, gates = _inproj(x, mod_l, ln_pre[l, 1:2], w_att, w_gate)
        bias_a_l = _with_sink_column(bias_a, attn_sinks[l])
        y_a = _band_attention(att, ATT_COLS["a_q"], ATT_COLS["a_k"], ATT_COLS["a_v"], bias_a_l, A_WINDOW, True)
        pos_rows, w1_groups = _compress_params(cmp_pos[l], cmp_w1[l])
        kc, vc = _compress(_compress_rows(att, "b_k_cmp"), _compress_rows(att, "b_v_cmp"), pos_rows, w1_groups,
                           cmp_w2[l].astype(BF16))
        o_cmp, sel = _cmpsel(att, kc, vc, overlap)
        o_slc = _slc_attention(att, sel, bias_near, bias_far, n_sel)
        o_win = _band_attention(att, ATT_COLS["b_q"], ATT_COLS["b_k_win"], ATT_COLS["b_v_win"], bias_w, B_WINDOW,
                                False)
        y_c = _stick_attention(att)
        x = _merge(x, mod_l, ln_post[l, 1:2], y_a, o_cmp, o_slc, o_win, y_c, gates,
                   w_branch[l].astype(BF16), w_out[l].astype(BF16))

        x = _ffn(x, mod_l, ln_pre[l, 2:3], ln_post[l, 2:3], ffn_w_gate[l, 1].astype(BF16),
                 ffn_w_up[l, 1].astype(BF16), ffn_w_down[l, 1].astype(BF16), sub=2)
    return x
```

```python
import functools
import math

import numpy as np
import jax
import jax.numpy as jnp
from jax import lax
from jax.experimental import pallas as pl
from jax.experimental.pallas import tpu as pltpu

F32 = jnp.float32
BF16 = jnp.bfloat16

HEAD_DIM = 64
A_HEADS, A_KV, A_WINDOW = 8, 2, 128
B_HEADS, B_KV, B_WINDOW = 8, 2, 512
CMP_BLOCK, CMP_STRIDE, CMP_HIDDEN = 32, 16, 128
SEL_BLOCK, SEL_TOPN = 64, 16
C_HEADS = 8
BRANCH_WIDTH = 512
N_BRANCH = 3
REL_BUCKETS, REL_MAX_DIST = 32, 128
N_SUB = 3
EPS = 1e-6
FFN_RES = 0.5
NEG = -1e30

LANES = 128
QB = 128
SLC_TILE = 256
STICK_TILE = 256
VMEM_LIMIT = 56 * 1024 * 1024

ATT_COLS = {
    "a_q": 0, "b_q": 512, "c_q": 1024, "c_k": 1536, "c_v": 2048,
    "a_k": 2560, "a_v": 2688, "b_k_cmp": 2816, "b_v_cmp": 2944,
    "b_k_slc": 3072, "b_v_slc": 3200, "b_k_win": 3328, "b_v_win": 3456,
}
ATT_W = 3584
MERGE_W = 3072
GATE_W = MERGE_W + LANES


def _src_layout(d_model):
    widths = [
        ("a_q", A_HEADS * HEAD_DIM), ("a_k", A_KV * HEAD_DIM), ("a_v", A_KV * HEAD_DIM),
        ("b_q", B_HEADS * HEAD_DIM),
        ("b_k_cmp", B_KV * HEAD_DIM), ("b_v_cmp", B_KV * HEAD_DIM),
        ("b_k_slc", B_KV * HEAD_DIM), ("b_v_slc", B_KV * HEAD_DIM),
        ("b_k_win", B_KV * HEAD_DIM), ("b_v_win", B_KV * HEAD_DIM),
        ("b_gate", 3 * B_HEADS),
        ("c_q", C_HEADS * HEAD_DIM), ("c_k", C_HEADS * HEAD_DIM), ("c_v", C_HEADS * HEAD_DIM),
        ("merge_gate", N_BRANCH * d_model),
    ]
    out, off = {}, 0
    for name, w in widths:
        out[name] = (off, off + w)
        off += w
    return out


def _cparams(sem):
    return pltpu.CompilerParams(dimension_semantics=sem, vmem_limit_bytes=VMEM_LIMIT)


def _resident(shape, index_map):
    return pl.BlockSpec(shape, index_map, pipeline_mode=pl.Buffered(1))


def _rms(x, gain):
    return x * lax.rsqrt(jnp.mean(x * x, axis=-1, keepdims=True) + EPS) * gain


def _modulated(x, mod_ref, sub, gain):
    shift = mod_ref[0, 3 * sub:3 * sub + 1, :]
    scale = mod_ref[0, 3 * sub + 1:3 * sub + 2, :]
    return _rms(x, gain) * (1.0 + scale) + shift


def _dot(a, b):
    return jnp.dot(a, b, preferred_element_type=F32)


def _dot_nt(a, b):
    return lax.dot_general(a, b, (((1,), (1,)), ((), ())), preferred_element_type=F32)


def _split3(x):
    hi = x.astype(BF16)
    r1 = x - hi.astype(F32)
    mid = r1.astype(BF16)
    lo = (r1 - mid.astype(F32)).astype(BF16)
    return hi, mid, lo


def _dot_exact(x, sel):
    hi, mid, lo = _split3(x)
    return _dot(hi, sel) + _dot(mid, sel) + _dot(lo, sel)


def _lane_half(shape):
    return (lax.broadcasted_iota(jnp.int32, shape, 1) % LANES) // HEAD_DIM


def _stack_group_queries(q, group, heads_per_group):
    half = _lane_half((q.shape[0], LANES))
    parts = []
    for hp in range(heads_per_group):
        h = group * heads_per_group + hp
        tile = q[:, (h // 2) * LANES:(h // 2 + 1) * LANES]
        if h % 2 != group:
            tile = pltpu.roll(tile, HEAD_DIM, 1)
        parts.append(jnp.where(half == group, tile, 0.0))
    scale = 1.0 / math.sqrt(HEAD_DIM)
    return (jnp.concatenate(parts, axis=0) * scale).astype(BF16)


def _unstack_group_outputs(pv, group, heads_per_group):
    tile = pv.shape[0] // heads_per_group
    half = _lane_half((tile, LANES))
    tiles = []
    for pair in range(heads_per_group // 2):
        even = pv[(2 * pair) * tile:(2 * pair + 1) * tile]
        odd = pv[(2 * pair + 1) * tile:(2 * pair + 2) * tile]
        if group != 0:
            even = pltpu.roll(even, HEAD_DIM, 1)
        if group != 1:
            odd = pltpu.roll(odd, HEAD_DIM, 1)
        tiles.append(jnp.where(half == 0, even, odd))
    return jnp.concatenate(tiles, axis=1)


def _mod_kernel(c_ref, w_ref, b_ref, o_ref):
    c = c_ref[...]
    act = (c * jax.nn.sigmoid(c)).astype(BF16)
    o_ref[0] = _dot(act, w_ref[0].astype(BF16)) + b_ref[0]


def _modulation(c, ada_w, ada_b):
    depth, d, n = ada_w.shape
    bsz = c.shape[0]
    tn = n // 4
    out = pl.pallas_call(
        _mod_kernel,
        grid=(depth, n // tn),
        in_specs=[
            pl.BlockSpec((bsz, d), lambda l, j: (0, 0)),
            pl.BlockSpec((1, d, tn), lambda l, j: (l, 0, j)),
            pl.BlockSpec((1, 1, tn), lambda l, j: (l, 0, j)),
        ],
        out_specs=pl.BlockSpec((1, bsz, tn), lambda l, j: (l, 0, j)),
        out_shape=jax.ShapeDtypeStruct((depth, bsz, n), F32),
        compiler_params=_cparams(("arbitrary", "arbitrary")),
        name="adaln_mod",
    )(c, ada_w, ada_b.reshape(depth, 1, n))
    return out.reshape(depth, bsz, N_SUB * 3, d)


def _ffn_kernel(x_ref, mod_ref, lnpre_ref, lnpost_ref, wg_ref, wu_ref, wd_ref, o_ref, acc_ref, *, sub, tf):
    x = x_ref[0]
    h = _modulated(x, mod_ref, sub, lnpre_ref[...]).astype(BF16)
    d_ff = wg_ref.shape[1]
    for j in range(d_ff // tf):
        g = _dot(h, wg_ref[:, j * tf:(j + 1) * tf])
        u = _dot(h, wu_ref[:, j * tf:(j + 1) * tf])
        a = (g * jax.nn.sigmoid(g) * u).astype(BF16)
        part = _dot(a, wd_ref[j * tf:(j + 1) * tf, :])
        if j == 0:
            acc_ref[...] = part
        else:
            acc_ref[...] += part
    gate = mod_ref[0, 3 * sub + 2:3 * sub + 3, :]
    o_ref[0] = x + FFN_RES * gate * _rms(acc_ref[...], lnpost_ref[...])


def _ffn(x, mod, ln_pre, ln_post, wg, wu, wd, sub, tm=512, tf=256):
    bsz, s, d = x.shape
    d_ff = wg.shape[1]
    row = lambda b, i: (b, i, 0)
    const2 = lambda b, i: (0, 0)
    return pl.pallas_call(
        functools.partial(_ffn_kernel, sub=sub, tf=tf),
        grid=(bsz, s // tm),
        in_specs=[
            pl.BlockSpec((1, tm, d), row),
            pl.BlockSpec((1, N_SUB * 3, d), lambda b, i: (b, 0, 0)),
            pl.BlockSpec((1, d), const2),
            pl.BlockSpec((1, d), const2),
            _resident((d, d_ff), const2),
            _resident((d, d_ff), const2),
            _resident((d_ff, d), const2),
        ],
        out_specs=pl.BlockSpec((1, tm, d), row),
        out_shape=jax.ShapeDtypeStruct(x.shape, F32),
        scratch_shapes=[pltpu.VMEM((tm, d), F32)],
        compiler_params=_cparams(("arbitrary", "arbitrary")),
        name="ffn",
    )(x, mod, ln_pre, ln_post, wg, wu, wd)


def _inproj_kernel(x_ref, mod_ref, lnpre_ref, watt_ref, wgate_ref, att_ref, gate_ref, *, att_chunk, gate_chunk):
    h = _modulated(x_ref[0], mod_ref, 1, lnpre_ref[...]).astype(BF16)
    for c0 in range(0, ATT_W, att_chunk):
        att_ref[0, :, c0:c0 + att_chunk] = _dot(h, watt_ref[:, c0:c0 + att_chunk]).astype(BF16)
    for c0 in range(0, GATE_W, gate_chunk):
        gate_ref[0, :, c0:c0 + gate_chunk] = _dot(h, wgate_ref[:, c0:c0 + gate_chunk]).astype(BF16)


def _inproj(x, mod, ln_pre, w_att, w_gate, tm=512):
    bsz, s, d = x.shape
    row = lambda b, i: (b, i, 0)
    const2 = lambda b, i: (0, 0)
    return pl.pallas_call(
        functools.partial(_inproj_kernel, att_chunk=512, gate_chunk=640),
        grid=(bsz, s // tm),
        in_specs=[
            pl.BlockSpec((1, tm, d), row),
            pl.BlockSpec((1, N_SUB * 3, d), lambda b, i: (b, 0, 0)),
            pl.BlockSpec((1, d), const2),
            _resident((d, ATT_W), const2),
            _resident((d, GATE_W), const2),
        ],
        out_specs=[pl.BlockSpec((1, tm, ATT_W), row), pl.BlockSpec((1, tm, GATE_W), row)],
        out_shape=[jax.ShapeDtypeStruct((bsz, s, ATT_W), BF16), jax.ShapeDtypeStruct((bsz, s, GATE_W), BF16)],
        compiler_params=_cparams(("arbitrary", "arbitrary")),
        name="inproj",
    )(x, mod, ln_pre, w_att, w_gate)


def _band_kernel(q_ref, k_ref, v_ref, bias_ref, o_ref, *, window, has_sink, q_tiles):
    nkb = window // QB + 1
    span = nkb * QB
    n_groups = bias_ref.shape[0]
    hpg = bias_ref.shape[1] // QB
    jk = lax.broadcasted_iota(jnp.int32, (1, span), 1)
    row0 = lax.broadcasted_iota(jnp.int32, (span, LANES), 0) == 0
    vhalf = _lane_half((span, LANES))
    chains = [(u, g) for u in range(q_tiles) for g in range(n_groups)]
    vspans, in_seqs, scores = [], [], []
    for u in range(q_tiles):
        i = pl.program_id(1) * q_tiles + u
        starts = [jnp.maximum(i - (nkb - 1) + kb, 0) * QB for kb in range(nkb)]
        kspan = jnp.concatenate([k_ref[0, pl.ds(pl.multiple_of(st, QB), QB), :] for st in starts], axis=0)
        vspan = jnp.concatenate([v_ref[0, pl.ds(pl.multiple_of(st, QB), QB), :] for st in starts], axis=0)
        in_seq = jk >= (nkb - 1 - i) * QB
        if has_sink:
            kspan = jnp.where(row0, jnp.zeros_like(kspan), kspan)
            vspan = jnp.where(row0, jnp.zeros_like(vspan), vspan)
            in_seq = in_seq | (jk == 0)
        vspans.append(vspan)
        in_seqs.append(in_seq)
        q = q_ref[0, u * QB:(u + 1) * QB, :].astype(F32)
        for g in range(n_groups):
            scores.append(_dot_nt(_stack_group_queries(q, g, hpg), kspan))
    weights = []
    for (u, g), z in zip(chains, scores):
        logits = jnp.where(in_seqs[u], z + bias_ref[g], NEG)
        weights.append(jnp.exp(logits - jnp.max(logits, axis=-1, keepdims=True)).astype(BF16))
    pvs = [_dot(e, jnp.where(vhalf == g, vspans[u], jnp.ones_like(vspans[u]))) for (u, g), e in zip(chains, weights)]
    for u in range(q_tiles):
        outs = []
        for g in range(n_groups):
            pv = pvs[u * n_groups + g]
            outs.append(_unstack_group_outputs(pv / pltpu.roll(pv, HEAD_DIM, 1), g, hpg))
        o_ref[0, u * QB:(u + 1) * QB, :] = jnp.concatenate(outs, axis=1).astype(BF16)


def _band_attention(att, q_col, k_col, v_col, bias, window, has_sink, q_tiles=2):
    bsz, s, _ = att.shape
    n_groups, rows, span = bias.shape
    width = n_groups * (rows // QB) * HEAD_DIM
    tq = q_tiles * QB
    return pl.pallas_call(
        functools.partial(_band_kernel, window=window, has_sink=has_sink, q_tiles=q_tiles),
        grid=(bsz, s // tq),
        in_specs=[
            pl.BlockSpec((1, tq, width), lambda b, i: (b, i, q_col // width)),
            pl.BlockSpec((1, s, LANES), lambda b, i: (b, 0, k_col // LANES)),
            pl.BlockSpec((1, s, LANES), lambda b, i: (b, 0, v_col // LANES)),
            _resident((n_groups, rows, span), lambda b, i: (0, 0, 0)),
        ],
        out_specs=pl.BlockSpec((1, tq, width), lambda b, i: (b, i, 0)),
        out_shape=jax.ShapeDtypeStruct((bsz, s, width), BF16),
        compiler_params=_cparams(("arbitrary", "arbitrary")),
        name=f"band{window}",
    )(att, att, att, bias)


def _compress_kernel(rk_ref, rv_ref, pos_ref, w1_ref, w2_ref, kc_ref, vc_ref):
    n_rows = rk_ref.shape[1]
    last = lax.broadcasted_iota(jnp.int32, (n_rows, 1), 0) == n_rows - 1
    for which, (r_ref, o_ref) in enumerate(((rk_ref, kc_ref), (rv_ref, vc_ref))):
        r = r_ref[0].astype(F32)
        lo = (r + pos_ref[which, 0]).astype(BF16)
        hi = (r + pos_ref[which, 1]).astype(BF16)
        halves = []
        for g in range(B_KV):
            hid = _dot(lo, w1_ref[which, 0, g]) + pltpu.roll(_dot(hi, w1_ref[which, 1, g]), n_rows - 1, 0)
            out = _dot(jax.nn.gelu(hid).astype(BF16), w2_ref[which])
            halves.append(jnp.where(last, 0.0, out))
        o_ref[0] = jnp.concatenate(halves, axis=1).astype(BF16)


def _compress(rk, rv, pos, w1, w2):
    bsz, n_rows, width = rk.shape
    g = B_KV
    r_spec = pl.BlockSpec((1, n_rows, width), lambda b: (b, 0, 0))
    o_spec = pl.BlockSpec((1, n_rows, g * HEAD_DIM), lambda b: (b, 0, 0))
    o_shape = jax.ShapeDtypeStruct((bsz, n_rows, g * HEAD_DIM), BF16)
    return pl.pallas_call(
        _compress_kernel,
        grid=(bsz,),
        in_specs=[
            r_spec, r_spec,
            pl.BlockSpec(pos.shape, lambda b: (0, 0, 0, 0)),
            pl.BlockSpec(w1.shape, lambda b: (0, 0, 0, 0, 0)),
            pl.BlockSpec(w2.shape, lambda b: (0, 0, 0)),
        ],
        out_specs=[o_spec, o_spec],
        out_shape=[o_shape, o_shape],
        compiler_params=_cparams(("arbitrary",)),
        name="nsa_compress",
    )(rk, rv, pos, w1, w2)


def _cmpsel_kernel(q_ref, kc_ref, vc_ref, ov_ref, o_ref, sel_ref):
    i = pl.program_id(1)
    q = q_ref[0].astype(F32)
    kc = kc_ref[0]
    vc = vc_ref[0]
    n_cmp_rows = kc.shape[0]
    hpg = B_HEADS // B_KV
    rows = hpg * QB
    t_stack = i * QB + lax.broadcasted_iota(jnp.int32, (rows, 1), 0) % QB
    n_idx = lax.broadcasted_iota(jnp.int32, (1, n_cmp_rows), 1)
    cmp_ok = (n_idx * CMP_STRIDE + CMP_BLOCK - 1 <= t_stack) & (n_idx < n_cmp_rows - 1)
    n_sel = ov_ref.shape[0]
    blk = lax.broadcasted_iota(jnp.int32, (n_sel, QB), 0)
    t = i * QB + lax.broadcasted_iota(jnp.int32, (n_sel, QB), 1)
    cur = t // SEL_BLOCK
    forced = (blk == 0) | (blk == cur) | (blk == cur - 1)
    future = blk * SEL_BLOCK > t
    ov = ov_ref[...]
    scores = [_dot_nt(_stack_group_queries(q, g, hpg), kc) for g in range(B_KV)]
    probs = []
    for z in scores:
        logits = jnp.where(cmp_ok, z, NEG)
        m = jnp.max(logits, axis=-1, keepdims=True)
        e = jnp.where(cmp_ok, jnp.exp(logits - m), 0.0)
        ssum = jnp.sum(e, axis=-1, keepdims=True)
        probs.append(e / jnp.where(ssum > 0, ssum, 1.0))
    outs = [_unstack_group_outputs(_dot(p.astype(BF16), vc), g, hpg) for g, p in enumerate(probs)]
    imps = []
    for p in probs:
        p_heads = p[0:QB]
        for hp in range(1, hpg):
            p_heads = p_heads + p[hp * QB:(hp + 1) * QB]
        hi, mid, lo = _split3(p_heads)
        imps.append(_dot_nt(ov, hi) + _dot_nt(ov, mid) + _dot_nt(ov, lo))
    sel_t = []
    for imp in imps:
        prio = jnp.where(forced, jnp.inf, jnp.where(future, -jnp.inf, imp))
        rank = jnp.zeros((n_sel, QB), F32)
        for c in range(n_sel):
            cand = prio[c:c + 1, :]
            rank = rank + jnp.where(blk > c, jnp.where(cand >= prio, 1.0, 0.0), jnp.where(cand > prio, 1.0, 0.0))
        sel_t.append(jnp.where(rank < min(SEL_TOPN, n_sel), 1.0, 0.0))
    sel_t.append(jnp.zeros((LANES - B_KV * n_sel, QB), F32))
    o_ref[0] = jnp.concatenate(outs, axis=1).astype(BF16)
    sel_ref[0] = jnp.concatenate(sel_t, axis=0).T.astype(BF16)


def _cmpsel(att, kc, vc, overlap):
    bsz, s, _ = att.shape
    n_rows = kc.shape[1]
    width = B_HEADS * HEAD_DIM
    return pl.pallas_call(
        _cmpsel_kernel,
        grid=(bsz, s // QB),
        in_specs=[
            pl.BlockSpec((1, QB, width), lambda b, i: (b, i, ATT_COLS["b_q"] // width)),
            pl.BlockSpec((1, n_rows, LANES), lambda b, i: (b, 0, 0)),
            pl.BlockSpec((1, n_rows, LANES), lambda b, i: (b, 0, 0)),
            pl.BlockSpec(overlap.shape, lambda b, i: (0, 0)),
        ],
        out_specs=[pl.BlockSpec((1, QB, width), lambda b, i: (b, i, 0)),
                   pl.BlockSpec((1, QB, LANES), lambda b, i: (b, i, 0))],
        out_shape=[jax.ShapeDtypeStruct((bsz, s, width), BF16), jax.ShapeDtypeStruct((bsz, s, LANES), BF16)],
        compiler_params=_cparams(("arbitrary", "arbitrary")),
        name="nsa_cmpsel",
    )(att, kc, vc, overlap)


def _slc_kernel(q_ref, k_ref, v_ref, sel_ref, bias_ref, far_ref, o_ref, *, n_sel):
    i = pl.program_id(1)
    hpg = B_HEADS // B_KV
    tile = sel_ref.shape[1]
    rows = hpg * tile
    q = q_ref[0].astype(F32)
    sel = sel_ref[0]
    sel_row = lax.broadcasted_iota(jnp.int32, (LANES, tile), 0)
    key_blk = lax.broadcasted_iota(jnp.int32, (LANES, tile), 1) // SEL_BLOCK
    iq = lax.broadcasted_iota(jnp.int32, (tile, tile), 0)
    jk = lax.broadcasted_iota(jnp.int32, (tile, tile), 1)
    vhalf = _lane_half((tile, LANES))
    qs = [_stack_group_queries(q, g, hpg) for g in range(B_KV)]

    def tile_step(j, carry, near, causal):
        st = pl.multiple_of(j * tile, tile)
        kt = k_ref[0, pl.ds(st, tile), :]
        vt = v_ref[0, pl.ds(st, tile), :]
        picked = []
        for g in range(B_KV):
            expand = jnp.where(sel_row == g * n_sel + j * (tile // SEL_BLOCK) + key_blk, 1.0, 0.0).astype(BF16)
            picked.append(_dot(sel, expand))
        zs = [_dot_nt(qs[g], kt) for g in range(B_KV)]
        es, m_news = [], []
        for g in range(B_KV):
            ok = picked[g] > 0.5
            if causal is not None:
                ok = ok & causal
            z = zs[g] + (far_ref[g] if near is None else bias_ref[g, :, near * tile:(near + 1) * tile])
            logits = jnp.concatenate(
                [jnp.where(ok, z[hp * tile:(hp + 1) * tile], NEG) for hp in range(hpg)], axis=0)
            m_new = jnp.maximum(carry[g][0], jnp.max(logits, axis=-1, keepdims=True))
            es.append(jnp.exp(logits - m_new).astype(BF16))
            m_news.append(m_new)
        new = []
        for g in range(B_KV):
            m, acc = carry[g]
            pv = _dot(es[g], jnp.where(vhalf == g, vt, jnp.ones_like(vt)))
            new.append((m_news[g], jnp.exp(m - m_news[g]) * acc + pv))
        return tuple(new)

    init = tuple((jnp.full((rows, 1), NEG, F32), jnp.zeros((rows, LANES), F32)) for _ in range(B_KV))
    carry = tile_step(i, init, 1, jk <= iq)
    carry = tile_step(jnp.maximum(i - 1, 0), carry, 0, (i >= 1) & (jk >= 0))
    carry = lax.fori_loop(0, jnp.maximum(i - 1, 0), lambda j, c: tile_step(j, c, None, None), carry)
    outs = [_unstack_group_outputs(acc / pltpu.roll(acc, HEAD_DIM, 1), g, hpg) for g, (_, acc) in enumerate(carry)]
    o_ref[0] = jnp.concatenate(outs, axis=1).astype(BF16)


def _slc_attention(att, sel, bias_near, bias_far, n_sel):
    bsz, s, _ = att.shape
    width = B_HEADS * HEAD_DIM
    n_groups, rows, span = bias_near.shape
    tile = span // 2
    return pl.pallas_call(
        functools.partial(_slc_kernel, n_sel=n_sel),
        grid=(bsz, s // tile),
        in_specs=[
            pl.BlockSpec((1, tile, width), lambda b, i: (b, i, ATT_COLS["b_q"] // width)),
            pl.BlockSpec((1, s, LANES), lambda b, i: (b, 0, ATT_COLS["b_k_slc"] // LANES)),
            pl.BlockSpec((1, s, LANES), lambda b, i: (b, 0, ATT_COLS["b_v_slc"] // LANES)),
            pl.BlockSpec((1, tile, LANES), lambda b, i: (b, i, 0)),
            _resident((n_groups, rows, span), lambda b, i: (0, 0, 0)),
            pl.BlockSpec((n_groups, rows, 1), lambda b, i: (0, 0, 0)),
        ],
        out_specs=pl.BlockSpec((1, tile, width), lambda b, i: (b, i, 0)),
        out_shape=jax.ShapeDtypeStruct((bsz, s, width), BF16),
        compiler_params=_cparams(("arbitrary", "arbitrary")),
        name="nsa_slc",
    )(att, att, att, sel, bias_near, bias_far)


def _stick_kernel(q_ref, k_ref, v_ref, o_ref):
    i = pl.program_id(1)
    pairs = C_HEADS // 2
    tile = q_ref.shape[1]
    rows = 2 * tile
    half = _lane_half((tile, LANES))
    strict = (lax.broadcasted_iota(jnp.int32, (rows, tile), 1)
              < lax.broadcasted_iota(jnp.int32, (rows, tile), 0) % tile)
    later_than = jnp.where(lax.broadcasted_iota(jnp.int32, (2 * tile, tile), 0) % tile
                           > lax.broadcasted_iota(jnp.int32, (2 * tile, tile), 1), 1.0, 0.0).astype(BF16)
    qs = []
    for p in range(pairs):
        q = q_ref[0, :, p * LANES:(p + 1) * LANES].astype(F32) * (1.0 / math.sqrt(HEAD_DIM))
        qs.append(jnp.concatenate([jnp.where(half == 0, q, 0.0), jnp.where(half == 1, q, 0.0)],
                                  axis=0).astype(BF16))

    def tile_step(j, carry, mask):
        st = pl.multiple_of(j * tile, tile)
        zs = [_dot_nt(qs[p], k_ref[0, pl.ds(st, tile), p * LANES:(p + 1) * LANES]) for p in range(pairs)]
        log_beta, log_keep, pieces = [], [], []
        for p in range(pairs):
            z = zs[p]
            lb = jnp.minimum(z, 0.0) - jnp.log(1.0 + jnp.exp(-jnp.abs(z)))
            lk = lb - z
            if mask is not None:
                lk = jnp.where(mask, lk, 0.0)
            hi = lk.astype(BF16)
            log_beta.append(lb)
            log_keep.append(lk)
            pieces.append(jnp.concatenate([hi, (lk - hi.astype(F32)).astype(BF16)], axis=1))
        later = [_dot(piece, later_than) for piece in pieces]
        ws = []
        for p in range(pairs):
            w = jnp.exp(log_beta[p] + later[p] + carry[p][0])
            if mask is not None:
                w = jnp.where(mask, w, 0.0)
            ws.append(w.astype(BF16))
        new = []
        for p in range(pairs):
            tail, acc = carry[p]
            vt = v_ref[0, pl.ds(st, tile), p * LANES:(p + 1) * LANES]
            new.append((tail + jnp.sum(log_keep[p], axis=-1, keepdims=True), acc + _dot(ws[p], vt)))
        return tuple(new)

    init = tuple((jnp.zeros((rows, 1), F32), jnp.zeros((rows, LANES), F32)) for _ in range(pairs))
    carry = tile_step(i, init, strict)
    carry = lax.fori_loop(0, i, lambda step, c: tile_step(i - 1 - step, c, None), carry)
    o_ref[0] = jnp.concatenate([jnp.where(half == 0, acc[0:tile], acc[tile:rows]) for _, acc in carry],
                               axis=1).astype(BF16)


def _stick_attention(att, tile=STICK_TILE):
    bsz, s, _ = att.shape
    width = C_HEADS * HEAD_DIM
    return pl.pallas_call(
        _stick_kernel,
        grid=(bsz, s // tile),
        in_specs=[
            pl.BlockSpec((1, tile, width), lambda b, i: (b, i, ATT_COLS["c_q"] // width)),
            pl.BlockSpec((1, s, width), lambda b, i: (b, 0, ATT_COLS["c_k"] // width)),
            pl.BlockSpec((1, s, width), lambda b, i: (b, 0, ATT_COLS["c_v"] // width)),
        ],
        out_specs=pl.BlockSpec((1, tile, width), lambda b, i: (b, i, 0)),
        out_shape=jax.ShapeDtypeStruct((bsz, s, width), BF16),
        compiler_params=_cparams(("arbitrary", "arbitrary")),
        name="stick",
    )(att, att, att)


def _merge_kernel(x_ref, mod_ref, lnpost_ref, ya_ref, oc_ref, os_ref, ow_ref, yc_ref, mg_ref, bg_ref,
                  wb_ref, wo_ref, o_ref):
    d = x_ref.shape[2]
    bw = ya_ref.shape[2]
    head_gate = jax.nn.sigmoid(bg_ref[0].astype(F32))
    gate_row = lax.broadcasted_iota(jnp.int32, (LANES, bw), 0)
    head_of_col = lax.broadcasted_iota(jnp.int32, (LANES, bw), 1) // HEAD_DIM
    y_b = jnp.zeros(ya_ref.shape[1:], F32)
    for br, ref in enumerate((oc_ref, os_ref, ow_ref)):
        expand = jnp.where(gate_row == br * B_HEADS + head_of_col, 1.0, 0.0).astype(BF16)
        y_b = y_b + _dot_exact(head_gate, expand) * ref[0].astype(F32)
    merged = jnp.zeros((x_ref.shape[1], d), F32)
    for br, y in enumerate((ya_ref[0], y_b.astype(BF16), yc_ref[0])):
        merged = merged + jax.nn.sigmoid(mg_ref[0, :, br * d:(br + 1) * d].astype(F32)) * _dot(y, wb_ref[br])
    t = _dot(merged.astype(BF16), wo_ref[...])
    gate = mod_ref[0, 5:6, :]
    o_ref[0] = x_ref[0] + gate * _rms(t, lnpost_ref[...])


def _merge(x, mod, ln_post, y_a, o_cmp, o_slc, o_win, y_c, gates, w_branch, w_out, tm=256):
    bsz, s, d = x.shape
    bw = y_a.shape[2]
    row = lambda b, i: (b, i, 0)
    const2 = lambda b, i: (0, 0)
    branch = pl.BlockSpec((1, tm, bw), row)
    return pl.pallas_call(
        _merge_kernel,
        grid=(bsz, s // tm),
        in_specs=[
            pl.BlockSpec((1, tm, d), row),
            pl.BlockSpec((1, N_SUB * 3, d), lambda b, i: (b, 0, 0)),
            pl.BlockSpec((1, d), const2),
            branch, branch, branch, branch, branch,
            pl.BlockSpec((1, tm, MERGE_W), row),
            pl.BlockSpec((1, tm, LANES), lambda b, i: (b, i, MERGE_W // LANES)),
            _resident(w_branch.shape, lambda b, i: (0, 0, 0)),
            _resident(w_out.shape, const2),
        ],
        out_specs=pl.BlockSpec((1, tm, d), row),
        out_shape=jax.ShapeDtypeStruct(x.shape, F32),
        compiler_params=_cparams(("arbitrary", "arbitrary")),
        name="merge",
    )(x, mod, ln_post, y_a, o_cmp, o_slc, o_win, y_c, gates, gates, w_branch, w_out)


def _t5_bucket(dist):
    max_exact = REL_BUCKETS // 2
    d = jnp.maximum(dist, 0)
    ratio = jnp.log(jnp.maximum(d, 1).astype(F32) / max_exact) / math.log(REL_MAX_DIST / max_exact)
    large = jnp.minimum(max_exact + (ratio * (REL_BUCKETS - max_exact)).astype(jnp.int32), REL_BUCKETS - 1)
    return jnp.where(d < max_exact, d, large)


def _band_bias(table, window, n_groups, mask_band=True, tile=QB):
    span = window + tile
    dist = window + jnp.arange(tile)[:, None] - jnp.arange(span)[None, :]
    onehot = (_t5_bucket(dist)[..., None] == jnp.arange(REL_BUCKETS)).astype(F32)
    bias = jnp.einsum('qkb,bh->hqk', onehot, table.astype(F32), precision=lax.Precision.HIGHEST)
    if mask_band:
        bias = jnp.where(((dist >= 0) & (dist < window))[None], bias, NEG)
    return bias.reshape(n_groups, -1, span)


def _with_sink_column(bias, sinks):
    n_groups, rows, _ = bias.shape
    col = jnp.repeat(sinks.astype(F32), QB).reshape(n_groups, rows)
    return bias.at[:, :, 0].set(col)


def _far_bucket_is_constant():
    d = np.arange(QB + 1, 1 << 16).astype(np.float32)
    max_exact = REL_BUCKETS // 2
    ratio = np.log(d / max_exact) / math.log(REL_MAX_DIST / max_exact)
    return bool(np.all(max_exact + (ratio * (REL_BUCKETS - max_exact)).astype(np.int32) >= REL_BUCKETS - 1))


def _overlap_matrix(n_rows, n_sel):
    cstart = np.arange(n_rows)[None, :] * CMP_STRIDE
    sstart = np.arange(n_sel)[:, None] * SEL_BLOCK
    ov = (cstart < sstart + SEL_BLOCK) & (cstart + CMP_BLOCK > sstart) & (np.arange(n_rows)[None, :] < n_rows - 1)
    return jnp.asarray(ov, dtype=BF16)


def _layer_weights(w_in_l, d):
    src = _src_layout(d)
    order = sorted(ATT_COLS, key=ATT_COLS.get)
    w_att = jnp.concatenate([w_in_l[:, src[n][0]:src[n][1]] for n in order], axis=1).astype(BF16)
    g0, g1 = src["b_gate"]
    m0, m1 = src["merge_gate"]
    pad = jnp.zeros((d, LANES - (g1 - g0)), w_in_l.dtype)
    w_gate = jnp.concatenate([w_in_l[:, m0:m1], w_in_l[:, g0:g1], pad], axis=1).astype(BF16)
    return w_att, w_gate


def _compress_rows(att, name):
    bsz, s, _ = att.shape
    c0 = ATT_COLS[name]
    return att[:, :, c0:c0 + LANES].reshape(bsz, s // CMP_STRIDE, CMP_STRIDE * LANES)


def _compress_params(pos, w1):
    hidden = w1.shape[-1]
    pos_rows = jnp.broadcast_to(pos.reshape(2, 2, CMP_STRIDE, 1, HEAD_DIM).astype(F32),
                                (2, 2, CMP_STRIDE, B_KV, HEAD_DIM)).reshape(2, 2, 1, CMP_STRIDE * LANES)
    w1r = w1.reshape(2, 2, CMP_STRIDE, HEAD_DIM, hidden)
    w1g = jnp.einsum('whjdc,gk->whgjkdc', w1r, jnp.eye(B_KV, dtype=w1.dtype))
    return pos_rows, w1g.reshape(2, 2, B_KV, CMP_STRIDE * LANES, hidden).astype(BF16)


def kernel(x, c, rel_bias, ada_w, ada_b, ln_pre, ln_post, ffn_w_gate, ffn_w_up, ffn_w_down, w_in, attn_sinks,
           cmp_pos, cmp_w1, cmp_w2, w_branch, w_out):
    bsz, s, d = x.shape
    depth = ada_w.shape[0]
    assert s % QB == 0 and s // CMP_STRIDE == LANES and CMP_BLOCK == 2 * CMP_STRIDE
    assert d == MERGE_W // N_BRANCH and A_WINDOW == QB and _far_bucket_is_constant()
    n_rows = s // CMP_STRIDE
    n_sel = s // SEL_BLOCK
    hpg_a, hpg_b = A_HEADS // A_KV, B_HEADS // B_KV

    mod = _modulation(c, ada_w, ada_b)
    table_a = rel_bias[:, :A_HEADS]
    table_b = rel_bias[:, A_HEADS:A_HEADS + B_HEADS]
    bias_a = _band_bias(table_a, A_WINDOW, A_KV)
    bias_w = _band_bias(table_b, B_WINDOW, B_KV)
    bias_near = _band_bias(table_b, SLC_TILE, B_KV, mask_band=False, tile=SLC_TILE)
    bias_far = jnp.repeat(table_b[REL_BUCKETS - 1].astype(F32), SLC_TILE).reshape(B_KV, hpg_b * SLC_TILE, 1)
    overlap = _overlap_matrix(n_rows, n_sel)

    for l in range(depth):
        mod_l = mod[l]
        x = _ffn(x, mod_l, ln_pre[l, 0:1], ln_post[l, 0:1], ffn_w_gate[l, 0].astype(BF16),
                 ffn_w_up[l, 0].astype(BF16), ffn_w_down[l, 0].astype(BF16), sub=0)

        w_att, w_gate = _layer_weights(w_in[l], d)
        att, gates = _inproj(x, mod_l, ln_pre[l, 1:2], w_att, w_gate)
        bias_a_l = _with_sink_column(bias_a, attn_sinks[l])
        y_a = _band_attention(att, ATT_COLS["a_q"], ATT_COLS["a_k"], ATT_COLS["a_v"], bias_a_l, A_WINDOW, True)
        pos_rows, w1_groups = _compress_params(cmp_pos[l], cmp_w1[l])
        kc, vc = _compress(_compress_rows(att, "b_k_cmp"), _compress_rows(att, "b_v_cmp"), pos_rows, w1_groups,
                           cmp_w2[l].astype(BF16))
        o_cmp, sel = _cmpsel(att, kc, vc, overlap)
        o_slc = _slc_attention(att, sel, bias_near, bias_far, n_sel)
        o_win = _band_attention(att, ATT_COLS["b_q"], ATT_COLS["b_k_win"], ATT_COLS["b_v_win"], bias_w, B_WINDOW,
                                False)
        y_c = _stick_attention(att)
        x = _merge(x, mod_l, ln_post[l, 1:2], y_a, o_cmp, o_slc, o_win, y_c, gates,
                   w_branch[l].astype(BF16), w_out[l].astype(BF16))

        x = _ffn(x, mod_l, ln_pre[l, 2:3], ln_post[l, 2:3], ffn_w_gate[l, 1].astype(BF16),
                 ffn_w_up[l, 1].astype(BF16), ffn_w_down[l, 1].astype(BF16), sub=2)
    return x
```

```python
import functools
import math

import numpy as np
import jax
import jax.numpy as jnp
from jax import lax
from jax.experimental import pallas as pl
from jax.experimental.pallas import tpu as pltpu

F32 = jnp.float32
BF16 = jnp.bfloat16

HEAD_DIM = 64
A_HEADS, A_KV, A_WINDOW = 8, 2, 128
B_HEADS, B_KV, B_WINDOW = 8, 2, 512
CMP_BLOCK, CMP_STRIDE, CMP_HIDDEN = 32, 16, 128
SEL_BLOCK, SEL_TOPN = 64, 16
C_HEADS = 8
BRANCH_WIDTH = 512
N_BRANCH = 3
REL_BUCKETS, REL_MAX_DIST = 32, 128
N_SUB = 3
EPS = 1e-6
FFN_RES = 0.5
NEG = -1e30

LANES = 128
QB = 128
SLC_TILE = 256
STICK_TILE = 256
VMEM_LIMIT = 56 * 1024 * 1024

ATT_COLS = {
    "a_q": 0, "b_q": 512, "c_q": 1024, "c_k": 1536, "c_v": 2048,
    "a_k": 2560, "a_v": 2688, "b_k_cmp": 2816, "b_v_cmp": 2944,
    "b_k_slc": 3072, "b_v_slc": 3200, "b_k_win": 3328, "b_v_win": 3456,
}
ATT_W = 3584
MERGE_W = 3072
GATE_W = MERGE_W + LANES


def _src_layout(d_model):
    widths = [
        ("a_q", A_HEADS * HEAD_DIM), ("a_k", A_KV * HEAD_DIM), ("a_v", A_KV * HEAD_DIM),
        ("b_q", B_HEADS * HEAD_DIM),
        ("b_k_cmp", B_KV * HEAD_DIM), ("b_v_cmp", B_KV * HEAD_DIM),
        ("b_k_slc", B_KV * HEAD_DIM), ("b_v_slc", B_KV * HEAD_DIM),
        ("b_k_win", B_KV * HEAD_DIM), ("b_v_win", B_KV * HEAD_DIM),
        ("b_gate", 3 * B_HEADS),
        ("c_q", C_HEADS * HEAD_DIM), ("c_k", C_HEADS * HEAD_DIM), ("c_v", C_HEADS * HEAD_DIM),
        ("merge_gate", N_BRANCH * d_model),
    ]
    out, off = {}, 0
    for name, w in widths:
        out[name] = (off, off + w)
        off += w
    return out


def _cparams(sem):
    return pltpu.CompilerParams(dimension_semantics=sem, vmem_limit_bytes=VMEM_LIMIT)


def _resident(shape, index_map):
    return pl.BlockSpec(shape, index_map, pipeline_mode=pl.Buffered(1))


def _rms(x, gain):
    return x * lax.rsqrt(jnp.mean(x * x, axis=-1, keepdims=True) + EPS) * gain


def _modulated(x, mod_ref, sub, gain):
    shift = mod_ref[0, 3 * sub:3 * sub + 1, :]
    scale = mod_ref[0, 3 * sub + 1:3 * sub + 2, :]
    return _rms(x, gain) * (1.0 + scale) + shift


def _dot(a, b):
    return jnp.dot(a, b, preferred_element_type=F32)


def _dot_nt(a, b):
    return lax.dot_general(a, b, (((1,), (1,)), ((), ())), preferred_element_type=F32)


def _split3(x):
    hi = x.astype(BF16)
    r1 = x - hi.astype(F32)
    mid = r1.astype(BF16)
    lo = (r1 - mid.astype(F32)).astype(BF16)
    return hi, mid, lo


def _dot_exact(x, sel):
    hi, mid, lo = _split3(x)
    return _dot(hi, sel) + _dot(mid, sel) + _dot(lo, sel)


def _lane_half(shape):
    return (lax.broadcasted_iota(jnp.int32, shape, 1) % LANES) // HEAD_DIM


def _stack_group_queries(q, group, heads_per_group):
    half = _lane_half((q.shape[0], LANES))
    parts = []
    for hp in range(heads_per_group):
        h = group * heads_per_group + hp
        tile = q[:, (h // 2) * LANES:(h // 2 + 1) * LANES]
        if h % 2 != group:
            tile = pltpu.roll(tile, HEAD_DIM, 1)
        parts.append(jnp.where(half == group, tile, 0.0))
    scale = 1.0 / math.sqrt(HEAD_DIM)
    return (jnp.concatenate(parts, axis=0) * scale).astype(BF16)


def _unstack_group_outputs(pv, group, heads_per_group):
    tile = pv.shape[0] // heads_per_group
    half = _lane_half((tile, LANES))
    tiles = []
    for pair in range(heads_per_group // 2):
        even = pv[(2 * pair) * tile:(2 * pair + 1) * tile]
        odd = pv[(2 * pair + 1) * tile:(2 * pair + 2) * tile]
        if group != 0:
            even = pltpu.roll(even, HEAD_DIM, 1)
        if group != 1:
            odd = pltpu.roll(odd, HEAD_DIM, 1)
        tiles.append(jnp.where(half == 0, even, odd))
    return jnp.concatenate(tiles, axis=1)


def _mod_kernel(c_ref, w_ref, b_ref, o_ref):
    c = c_ref[...]
    act = (c * jax.nn.sigmoid(c)).astype(BF16)
    o_ref[0] = _dot(act, w_ref[0].astype(BF16)) + b_ref[0]


def _modulation(c, ada_w, ada_b):
    depth, d, n = ada_w.shape
    bsz = c.shape[0]
    tn = n // 4
    out = pl.pallas_call(
        _mod_kernel,
        grid=(depth, n // tn),
        in_specs=[
            pl.BlockSpec((bsz, d), lambda l, j: (0, 0)),
            pl.BlockSpec((1, d, tn), lambda l, j: (l, 0, j)),
            pl.BlockSpec((1, 1, tn), lambda l, j: (l, 0, j)),
        ],
        out_specs=pl.BlockSpec((1, bsz, tn), lambda l, j: (l, 0, j)),
        out_shape=jax.ShapeDtypeStruct((depth, bsz, n), F32),
        compiler_params=_cparams(("arbitrary", "arbitrary")),
        name="adaln_mod",
    )(c, ada_w, ada_b.reshape(depth, 1, n))
    return out.reshape(depth, bsz, N_SUB * 3, d)


def _ffn_kernel(x_ref, mod_ref, lnpre_ref, lnpost_ref, wg_ref, wu_ref, wd_ref, o_ref, acc_ref, *, sub, tf):
    x = x_ref[0]
    h = _modulated(x, mod_ref, sub, lnpre_ref[...]).astype(BF16)
    d_ff = wg_ref.shape[1]
    for j in range(d_ff // tf):
        g = _dot(h, wg_ref[:, j * tf:(j + 1) * tf])
        u = _dot(h, wu_ref[:, j * tf:(j + 1) * tf])
        a = (g * jax.nn.sigmoid(g) * u).astype(BF16)
        part = _dot(a, wd_ref[j * tf:(j + 1) * tf, :])
        if j == 0:
            acc_ref[...] = part
        else:
            acc_ref[...] += part
    gate = mod_ref[0, 3 * sub + 2:3 * sub + 3, :]
    o_ref[0] = x + FFN_RES * gate * _rms(acc_ref[...], lnpost_ref[...])


def _ffn(x, mod, ln_pre, ln_post, wg, wu, wd, sub, tm=512, tf=256):
    bsz, s, d = x.shape
    d_ff = wg.shape[1]
    row = lambda b, i: (b, i, 0)
    const2 = lambda b, i: (0, 0)
    return pl.pallas_call(
        functools.partial(_ffn_kernel, sub=sub, tf=tf),
        grid=(bsz, s // tm),
        in_specs=[
            pl.BlockSpec((1, tm, d), row),
            pl.BlockSpec((1, N_SUB * 3, d), lambda b, i: (b, 0, 0)),
            pl.BlockSpec((1, d), const2),
            pl.BlockSpec((1, d), const2),
            _resident((d, d_ff), const2),
            _resident((d, d_ff), const2),
            _resident((d_ff, d), const2),
        ],
        out_specs=pl.BlockSpec((1, tm, d), row),
        out_shape=jax.ShapeDtypeStruct(x.shape, F32),
        scratch_shapes=[pltpu.VMEM((tm, d), F32)],
        compiler_params=_cparams(("arbitrary", "arbitrary")),
        name="ffn",
    )(x, mod, ln_pre, ln_post, wg, wu, wd)


def _inproj_kernel(x_ref, mod_ref, lnpre_ref, watt_ref, wgate_ref, att_ref, gate_ref, *, att_chunk, gate_chunk):
    h = _modulated(x_ref[0], mod_ref, 1, lnpre_ref[...]).astype(BF16)
    for c0 in range(0, ATT_W, att_chunk):
        att_ref[0, :, c0:c0 + att_chunk] = _dot(h, watt_ref[:, c0:c0 + att_chunk]).astype(BF16)
    for c0 in range(0, GATE_W, gate_chunk):
        gate_ref[0, :, c0:c0 + gate_chunk] = _dot(h, wgate_ref[:, c0:c0 + gate_chunk]).astype(BF16)


def _inproj(x, mod, ln_pre, w_att, w_gate, tm=512):
    bsz, s, d = x.shape
    row = lambda b, i: (b, i, 0)
    const2 = lambda b, i: (0, 0)
    return pl.pallas_call(
        functools.partial(_inproj_kernel, att_chunk=512, gate_chunk=640),
        grid=(bsz, s // tm),
        in_specs=[
            pl.BlockSpec((1, tm, d), row),
            pl.BlockSpec((1, N_SUB * 3, d), lambda b, i: (b, 0, 0)),
            pl.BlockSpec((1, d), const2),
            _resident((d, ATT_W), const2),
            _resident((d, GATE_W), const2),
        ],
        out_specs=[pl.BlockSpec((1, tm, ATT_W), row), pl.BlockSpec((1, tm, GATE_W), row)],
        out_shape=[jax.ShapeDtypeStruct((bsz, s, ATT_W), BF16), jax.ShapeDtypeStruct((bsz, s, GATE_W), BF16)],
        compiler_params=_cparams(("arbitrary", "arbitrary")),
        name="inproj",
    )(x, mod, ln_pre, w_att, w_gate)


def _band_kernel(q_ref, k_ref, v_ref, bias_ref, o_ref, *, window, has_sink, q_tiles):
    nkb = window // QB + 1
    span = nkb * QB
    n_groups = bias_ref.shape[0]
    hpg = bias_ref.shape[1] // QB
    jk = lax.broadcasted_iota(jnp.int32, (1, span), 1)
    row0 = lax.broadcasted_iota(jnp.int32, (span, LANES), 0) == 0
    vhalf = _lane_half((span, LANES))
    chains = [(u, g) for u in range(q_tiles) for g in range(n_groups)]
    vspans, in_seqs, scores = [], [], []
    for u in range(q_tiles):
        i = pl.program_id(1) * q_tiles + u
        starts = [jnp.maximum(i - (nkb - 1) + kb, 0) * QB for kb in range(nkb)]
        kspan = jnp.concatenate([k_ref[0, pl.ds(pl.multiple_of(st, QB), QB), :] for st in starts], axis=0)
        vspan = jnp.concatenate([v_ref[0, pl.ds(pl.multiple_of(st, QB), QB), :] for st in starts], axis=0)
        in_seq = jk >= (nkb - 1 - i) * QB
        if has_sink:
            kspan = jnp.where(row0, jnp.zeros_like(kspan), kspan)
            vspan = jnp.where(row0, jnp.zeros_like(vspan), vspan)
            in_seq = in_seq | (jk == 0)
        vspans.append(vspan)
        in_seqs.append(in_seq)
        q = q_ref[0, u * QB:(u + 1) * QB, :].astype(F32)
        for g in range(n_groups):
            scores.append(_dot_nt(_stack_group_queries(q, g, hpg), kspan))
    weights = []
    for (u, g), z in zip(chains, scores):
        logits = jnp.where(in_seqs[u], z + bias_ref[g], NEG)
        weights.append(jnp.exp(logits - jnp.max(logits, axis=-1, keepdims=True)).astype(BF16))
    pvs = [_dot(e, jnp.where(vhalf == g, vspans[u], jnp.ones_like(vspans[u]))) for (u, g), e in zip(chains, weights)]
    for u in range(q_tiles):
        outs = []
        for g in range(n_groups):
            pv = pvs[u * n_groups + g]
            outs.append(_unstack_group_outputs(pv / pltpu.roll(pv, HEAD_DIM, 1), g, hpg))
        o_ref[0, u * QB:(u + 1) * QB, :] = jnp.concatenate(outs, axis=1).astype(BF16)


def _band_attention(att, q_col, k_col, v_col, bias, window, has_sink, q_tiles=4):
    bsz, s, _ = att.shape
    n_groups, rows, span = bias.shape
    width = n_groups * (rows // QB) * HEAD_DIM
    tq = q_tiles * QB
    return pl.pallas_call(
        functools.partial(_band_kernel, window=window, has_sink=has_sink, q_tiles=q_tiles),
        grid=(bsz, s // tq),
        in_specs=[
            pl.BlockSpec((1, tq, width), lambda b, i: (b, i, q_col // width)),
            pl.BlockSpec((1, s, LANES), lambda b, i: (b, 0, k_col // LANES)),
            pl.BlockSpec((1, s, LANES), lambda b, i: (b, 0, v_col // LANES)),
            _resident((n_groups, rows, span), lambda b, i: (0, 0, 0)),
        ],
        out_specs=pl.BlockSpec((1, tq, width), lambda b, i: (b, i, 0)),
        out_shape=jax.ShapeDtypeStruct((bsz, s, width), BF16),
        compiler_params=_cparams(("arbitrary", "arbitrary")),
        name=f"band{window}",
    )(att, att, att, bias)


def _compress_kernel(rk_ref, rv_ref, pos_ref, w1_ref, w2_ref, kc_ref, vc_ref):
    n_rows = rk_ref.shape[1]
    last = lax.broadcasted_iota(jnp.int32, (n_rows, 1), 0) == n_rows - 1
    for which, (r_ref, o_ref) in enumerate(((rk_ref, kc_ref), (rv_ref, vc_ref))):
        r = r_ref[0].astype(F32)
        lo = (r + pos_ref[which, 0]).astype(BF16)
        hi = (r + pos_ref[which, 1]).astype(BF16)
        halves = []
        for g in range(B_KV):
            hid = _dot(lo, w1_ref[which, 0, g]) + pltpu.roll(_dot(hi, w1_ref[which, 1, g]), n_rows - 1, 0)
            out = _dot(jax.nn.gelu(hid).astype(BF16), w2_ref[which])
            halves.append(jnp.where(last, 0.0, out))
        o_ref[0] = jnp.concatenate(halves, axis=1).astype(BF16)


def _compress(rk, rv, pos, w1, w2):
    bsz, n_rows, width = rk.shape
    g = B_KV
    r_spec = pl.BlockSpec((1, n_rows, width), lambda b: (b, 0, 0))
    o_spec = pl.BlockSpec((1, n_rows, g * HEAD_DIM), lambda b: (b, 0, 0))
    o_shape = jax.ShapeDtypeStruct((bsz, n_rows, g * HEAD_DIM), BF16)
    return pl.pallas_call(
        _compress_kernel,
        grid=(bsz,),
        in_specs=[
            r_spec, r_spec,
            pl.BlockSpec(pos.shape, lambda b: (0, 0, 0, 0)),
            pl.BlockSpec(w1.shape, lambda b: (0, 0, 0, 0, 0)),
            pl.BlockSpec(w2.shape, lambda b: (0, 0, 0)),
        ],
        out_specs=[o_spec, o_spec],
        out_shape=[o_shape, o_shape],
        compiler_params=_cparams(("arbitrary",)),
        name="nsa_compress",
    )(rk, rv, pos, w1, w2)


def _cmpsel_kernel(q_ref, kc_ref, vc_ref, ov_ref, o_ref, sel_ref):
    i = pl.program_id(1)
    q = q_ref[0].astype(F32)
    kc = kc_ref[0]
    vc = vc_ref[0]
    n_cmp_rows = kc.shape[0]
    hpg = B_HEADS // B_KV
    rows = hpg * QB
    t_stack = i * QB + lax.broadcasted_iota(jnp.int32, (rows, 1), 0) % QB
    n_idx = lax.broadcasted_iota(jnp.int32, (1, n_cmp_rows), 1)
    cmp_ok = (n_idx * CMP_STRIDE + CMP_BLOCK - 1 <= t_stack) & (n_idx < n_cmp_rows - 1)
    n_sel = ov_ref.shape[0]
    blk = lax.broadcasted_iota(jnp.int32, (n_sel, QB), 0)
    t = i * QB + lax.broadcasted_iota(jnp.int32, (n_sel, QB), 1)
    cur = t // SEL_BLOCK
    forced = (blk == 0) | (blk == cur) | (blk == cur - 1)
    future = blk * SEL_BLOCK > t
    ov = ov_ref[...]
    scores = [_dot_nt(_stack_group_queries(q, g, hpg), kc) for g in range(B_KV)]
    probs = []
    for z in scores:
        logits = jnp.where(cmp_ok, z, NEG)
        m = jnp.max(logits, axis=-1, keepdims=True)
        e = jnp.where(cmp_ok, jnp.exp(logits - m), 0.0)
        ssum = jnp.sum(e, axis=-1, keepdims=True)
        probs.append(e / jnp.where(ssum > 0, ssum, 1.0))
    outs = [_unstack_group_outputs(_dot(p.astype(BF16), vc), g, hpg) for g, p in enumerate(probs)]
    imps = []
    for p in probs:
        p_heads = p[0:QB]
        for hp in range(1, hpg):
            p_heads = p_heads + p[hp * QB:(hp + 1) * QB]
        hi, mid, lo = _split3(p_heads)
        imps.append(_dot_nt(ov, hi) + _dot_nt(ov, mid) + _dot_nt(ov, lo))
    sel_t = []
    for imp in imps:
        prio = jnp.where(forced, jnp.inf, jnp.where(future, -jnp.inf, imp))
        rank = jnp.zeros((n_sel, QB), F32)
        for c in range(n_sel):
            cand = prio[c:c + 1, :]
            rank = rank + jnp.where(blk > c, jnp.where(cand >= prio, 1.0, 0.0), jnp.where(cand > prio, 1.0, 0.0))
        sel_t.append(jnp.where(rank < min(SEL_TOPN, n_sel), 1.0, 0.0))
    sel_t.append(jnp.zeros((LANES - B_KV * n_sel, QB), F32))
    o_ref[0] = jnp.concatenate(outs, axis=1).astype(BF16)
    sel_ref[0] = jnp.concatenate(sel_t, axis=0).T.astype(BF16)


def _cmpsel(att, kc, vc, overlap):
    bsz, s, _ = att.shape
    n_rows = kc.shape[1]
    width = B_HEADS * HEAD_DIM
    return pl.pallas_call(
        _cmpsel_kernel,
        grid=(bsz, s // QB),
        in_specs=[
            pl.BlockSpec((1, QB, width), lambda b, i: (b, i, ATT_COLS["b_q"] // width)),
            pl.BlockSpec((1, n_rows, LANES), lambda b, i: (b, 0, 0)),
            pl.BlockSpec((1, n_rows, LANES), lambda b, i: (b, 0, 0)),
            pl.BlockSpec(overlap.shape, lambda b, i: (0, 0)),
        ],
        out_specs=[pl.BlockSpec((1, QB, width), lambda b, i: (b, i, 0)),
                   pl.BlockSpec((1, QB, LANES), lambda b, i: (b, i, 0))],
        out_shape=[jax.ShapeDtypeStruct((bsz, s, width), BF16), jax.ShapeDtypeStruct((bsz, s, LANES), BF16)],
        compiler_params=_cparams(("arbitrary", "arbitrary")),
        name="nsa_cmpsel",
    )(att, kc, vc, overlap)


def _slc_kernel(q_ref, k_ref, v_ref, sel_ref, bias_ref, far_ref, o_ref, *, n_sel):
    i = pl.program_id(1)
    hpg = B_HEADS // B_KV
    tile = sel_ref.shape[1]
    rows = hpg * tile
    q = q_ref[0].astype(F32)
    sel = sel_ref[0]
    sel_row = lax.broadcasted_iota(jnp.int32, (LANES, tile), 0)
    key_blk = lax.broadcasted_iota(jnp.int32, (LANES, tile), 1) // SEL_BLOCK
    iq = lax.broadcasted_iota(jnp.int32, (tile, tile), 0)
    jk = lax.broadcasted_iota(jnp.int32, (tile, tile), 1)
    vhalf = _lane_half((tile, LANES))
    qs = [_stack_group_queries(q, g, hpg) for g in range(B_KV)]
    pair_rows = 2 * tile
    chains = [(g, r0) for g in range(B_KV) for r0 in range(0, rows, pair_rows)]

    def tile_step(j, carry, near, causal):
        st = pl.multiple_of(j * tile, tile)
        kt = k_ref[0, pl.ds(st, tile), :]
        vt = v_ref[0, pl.ds(st, tile), :]
        picked, zs, es, m_news, new = {}, {}, {}, {}, {}

        def stage(s, c):
            g, r0 = chains[c]
            if s == 0:
                if g not in picked:
                    expand = jnp.where(sel_row == g * n_sel + j * (tile // SEL_BLOCK) + key_blk, 1.0, 0.0)
                    ok = _dot(sel, expand.astype(BF16)) > 0.5
                    picked[g] = ok if causal is None else ok & causal
                zs[c] = _dot_nt(qs[g][r0:r0 + pair_rows], kt)
            elif s == 1:
                if near is None:
                    z = zs[c] + far_ref[g, r0:r0 + pair_rows]
                else:
                    z = zs[c] + bias_ref[g, r0:r0 + pair_rows, near * tile:(near + 1) * tile]
                logits = jnp.concatenate(
                    [jnp.where(picked[g], z[h * tile:(h + 1) * tile], NEG) for h in range(2)], axis=0)
                m_news[c] = jnp.maximum(carry[c][0], jnp.max(logits, axis=-1, keepdims=True))
                es[c] = jnp.exp(logits - m_news[c]).astype(BF16)
            else:
                m, acc = carry[c]
                pv = _dot(es[c], jnp.where(vhalf == g, vt, jnp.ones_like(vt)))
                new[c] = (m_news[c], jnp.exp(m - m_news[c]) * acc + pv)

        n_stages = 3
        for t in range(len(chains) + n_stages - 1):
            for s in range(n_stages):
                if 0 <= t - s < len(chains):
                    stage(s, t - s)
        return tuple(new[c] for c in range(len(chains)))

    init = tuple((jnp.full((pair_rows, 1), NEG, F32), jnp.zeros((pair_rows, LANES), F32)) for _ in chains)
    carry = tile_step(i, init, 1, jk <= iq)
    carry = tile_step(jnp.maximum(i - 1, 0), carry, 0, (i >= 1) & (jk >= 0))
    carry = lax.fori_loop(0, jnp.maximum(i - 1, 0), lambda j, c: tile_step(j, c, None, None), carry)
    outs = []
    for g in range(B_KV):
        acc = jnp.concatenate([carry[c][1] for c, (cg, _) in enumerate(chains) if cg == g], axis=0)
        outs.append(_unstack_group_outputs(acc / pltpu.roll(acc, HEAD_DIM, 1), g, hpg))
    o_ref[0] = jnp.concatenate(outs, axis=1).astype(BF16)


def _slc_attention(att, sel, bias_near, bias_far, n_sel):
    bsz, s, _ = att.shape
    width = B_HEADS * HEAD_DIM
    n_groups, rows, span = bias_near.shape
    tile = span // 2
    return pl.pallas_call(
        functools.partial(_slc_kernel, n_sel=n_sel),
        grid=(bsz, s // tile),
        in_specs=[
            pl.BlockSpec((1, tile, width), lambda b, i: (b, i, ATT_COLS["b_q"] // width)),
            pl.BlockSpec((1, s, LANES), lambda b, i: (b, 0, ATT_COLS["b_k_slc"] // LANES)),
            pl.BlockSpec((1, s, LANES), lambda b, i: (b, 0, ATT_COLS["b_v_slc"] // LANES)),
            pl.BlockSpec((1, tile, LANES), lambda b, i: (b, i, 0)),
            _resident((n_groups, rows, span), lambda b, i: (0, 0, 0)),
            pl.BlockSpec((n_groups, rows, 1), lambda b, i: (0, 0, 0)),
        ],
        out_specs=pl.BlockSpec((1, tile, width), lambda b, i: (b, i, 0)),
        out_shape=jax.ShapeDtypeStruct((bsz, s, width), BF16),
        compiler_params=_cparams(("arbitrary", "arbitrary")),
        name="nsa_slc",
    )(att, att, att, sel, bias_near, bias_far)


def _stick_kernel(q_ref, k_ref, v_ref, o_ref):
    i = pl.program_id(1)
    pairs = C_HEADS // 2
    tile = q_ref.shape[1]
    rows = 2 * tile
    half = _lane_half((tile, LANES))
    strict = (lax.broadcasted_iota(jnp.int32, (rows, tile), 1)
              < lax.broadcasted_iota(jnp.int32, (rows, tile), 0) % tile)
    later_than = jnp.where(lax.broadcasted_iota(jnp.int32, (2 * tile, tile), 0) % tile
                           > lax.broadcasted_iota(jnp.int32, (2 * tile, tile), 1), 1.0, 0.0).astype(BF16)
    qs = []
    for p in range(pairs):
        q = q_ref[0, :, p * LANES:(p + 1) * LANES].astype(F32) * (1.0 / math.sqrt(HEAD_DIM))
        qs.append(jnp.concatenate([jnp.where(half == 0, q, 0.0), jnp.where(half == 1, q, 0.0)],
                                  axis=0).astype(BF16))

    def tile_step(j, carry, mask):
        st = pl.multiple_of(j * tile, tile)
        zs, log_beta, log_keep, later, ws, new = {}, {}, {}, {}, {}, {}

        def stage(s, p):
            if s == 0:
                zs[p] = _dot_nt(qs[p], k_ref[0, pl.ds(st, tile), p * LANES:(p + 1) * LANES])
            elif s == 1:
                z = zs[p]
                neg_abs = pltpu.bitcast(pltpu.bitcast(z, jnp.uint32) | jnp.uint32(0x80000000), F32)
                lb = jnp.minimum(z, 0.0) - jnp.log(1.0 + jnp.exp(neg_abs))
                lk = lb - z
                if mask is not None:
                    lk = jnp.where(mask, lk, 0.0)
                log_beta[p], log_keep[p] = lb, lk
            elif s == 2:
                lk = log_keep[p]
                hi = lk.astype(BF16)
                later[p] = _dot(jnp.concatenate([hi, (lk - hi.astype(F32)).astype(BF16)], axis=1), later_than)
            elif s == 3:
                w = jnp.exp(log_beta[p] + later[p] + carry[p][0])
                if mask is not None:
                    w = jnp.where(mask, w, 0.0)
                ws[p] = w.astype(BF16)
            else:
                tail, acc = carry[p]
                vt = v_ref[0, pl.ds(st, tile), p * LANES:(p + 1) * LANES]
                new[p] = (tail + jnp.sum(log_keep[p], axis=-1, keepdims=True), acc + _dot(ws[p], vt))

        n_stages = 5
        for t in range(pairs + n_stages - 1):
            for s in range(n_stages):
                if 0 <= t - s < pairs:
                    stage(s, t - s)
        return tuple(new[p] for p in range(pairs))

    init = tuple((jnp.zeros((rows, 1), F32), jnp.zeros((rows, LANES), F32)) for _ in range(pairs))
    carry = tile_step(i, init, strict)
    carry = lax.fori_loop(0, i, lambda step, c: tile_step(i - 1 - step, c, None), carry)
    o_ref[0] = jnp.concatenate([jnp.where(half == 0, acc[0:tile], acc[tile:rows]) for _, acc in carry],
                               axis=1).astype(BF16)


def _stick_attention(att, tile=STICK_TILE):
    bsz, s, _ = att.shape
    width = C_HEADS * HEAD_DIM
    return pl.pallas_call(
        _stick_kernel,
        grid=(bsz, s // tile),
        in_specs=[
            pl.BlockSpec((1, tile, width), lambda b, i: (b, i, ATT_COLS["c_q"] // width)),
            pl.BlockSpec((1, s, width), lambda b, i: (b, 0, ATT_COLS["c_k"] // width)),
            pl.BlockSpec((1, s, width), lambda b, i: (b, 0, ATT_COLS["c_v"] // width)),
        ],
        out_specs=pl.BlockSpec((1, tile, width), lambda b, i: (b, i, 0)),
        out_shape=jax.ShapeDtypeStruct((bsz, s, width), BF16),
        compiler_params=_cparams(("arbitrary", "arbitrary")),
        name="stick",
    )(att, att, att)


def _merge_kernel(x_ref, mod_ref, lnpost_ref, ya_ref, oc_ref, os_ref, ow_ref, yc_ref, mg_ref, bg_ref,
                  wb_ref, wo_ref, o_ref):
    d = x_ref.shape[2]
    bw = ya_ref.shape[2]
    head_gate = jax.nn.sigmoid(bg_ref[0].astype(F32))
    gate_row = lax.broadcasted_iota(jnp.int32, (LANES, bw), 0)
    head_of_col = lax.broadcasted_iota(jnp.int32, (LANES, bw), 1) // HEAD_DIM
    y_b = jnp.zeros(ya_ref.shape[1:], F32)
    for br, ref in enumerate((oc_ref, os_ref, ow_ref)):
        expand = jnp.where(gate_row == br * B_HEADS + head_of_col, 1.0, 0.0).astype(BF16)
        y_b = y_b + _dot_exact(head_gate, expand) * ref[0].astype(F32)
    merged = jnp.zeros((x_ref.shape[1], d), F32)
    for br, y in enumerate((ya_ref[0], y_b.astype(BF16), yc_ref[0])):
        merged = merged + jax.nn.sigmoid(mg_ref[0, :, br * d:(br + 1) * d].astype(F32)) * _dot(y, wb_ref[br])
    t = _dot(merged.astype(BF16), wo_ref[...])
    gate = mod_ref[0, 5:6, :]
    o_ref[0] = x_ref[0] + gate * _rms(t, lnpost_ref[...])


def _merge(x, mod, ln_post, y_a, o_cmp, o_slc, o_win, y_c, gates, w_branch, w_out, tm=256):
    bsz, s, d = x.shape
    bw = y_a.shape[2]
    row = lambda b, i: (b, i, 0)
    const2 = lambda b, i: (0, 0)
    branch = pl.BlockSpec((1, tm, bw), row)
    return pl.pallas_call(
        _merge_kernel,
        grid=(bsz, s // tm),
        in_specs=[
            pl.BlockSpec((1, tm, d), row),
            pl.BlockSpec((1, N_SUB * 3, d), lambda b, i: (b, 0, 0)),
            pl.BlockSpec((1, d), const2),
            branch, branch, branch, branch, branch,
            pl.BlockSpec((1, tm, MERGE_W), row),
            pl.BlockSpec((1, tm, LANES), lambda b, i: (b, i, MERGE_W // LANES)),
            _resident(w_branch.shape, lambda b, i: (0, 0, 0)),
            _resident(w_out.shape, const2),
        ],
        out_specs=pl.BlockSpec((1, tm, d), row),
        out_shape=jax.ShapeDtypeStruct(x.shape, F32),
        compiler_params=_cparams(("arbitrary", "arbitrary")),
        name="merge",
    )(x, mod, ln_post, y_a, o_cmp, o_slc, o_win, y_c, gates, gates, w_branch, w_out)


def _t5_bucket(dist):
    max_exact = REL_BUCKETS // 2
    d = jnp.maximum(dist, 0)
    ratio = jnp.log(jnp.maximum(d, 1).astype(F32) / max_exact) / math.log(REL_MAX_DIST / max_exact)
    large = jnp.minimum(max_exact + (ratio * (REL_BUCKETS - max_exact)).astype(jnp.int32), REL_BUCKETS - 1)
    return jnp.where(d < max_exact, d, large)


def _band_bias(table, window, n_groups, mask_band=True, tile=QB):
    span = window + tile
    dist = window + jnp.arange(tile)[:, None] - jnp.arange(span)[None, :]
    onehot = (_t5_bucket(dist)[..., None] == jnp.arange(REL_BUCKETS)).astype(F32)
    bias = jnp.einsum('qkb,bh->hqk', onehot, table.astype(F32), precision=lax.Precision.HIGHEST)
    if mask_band:
        bias = jnp.where(((dist >= 0) & (dist < window))[None], bias, NEG)
    return bias.reshape(n_groups, -1, span)


def _with_sink_column(bias, sinks):
    n_groups, rows, _ = bias.shape
    col = jnp.repeat(sinks.astype(F32), QB).reshape(n_groups, rows)
    return bias.at[:, :, 0].set(col)


def _far_bucket_is_constant():
    d = np.arange(QB + 1, 1 << 16).astype(np.float32)
    max_exact = REL_BUCKETS // 2
    ratio = np.log(d / max_exact) / math.log(REL_MAX_DIST / max_exact)
    return bool(np.all(max_exact + (ratio * (REL_BUCKETS - max_exact)).astype(np.int32) >= REL_BUCKETS - 1))


def _overlap_matrix(n_rows, n_sel):
    cstart = np.arange(n_rows)[None, :] * CMP_STRIDE
    sstart = np.arange(n_sel)[:, None] * SEL_BLOCK
    ov = (cstart < sstart + SEL_BLOCK) & (cstart + CMP_BLOCK > sstart) & (np.arange(n_rows)[None, :] < n_rows - 1)
    return jnp.asarray(ov, dtype=BF16)


def _layer_weights(w_in_l, d):
    src = _src_layout(d)
    order = sorted(ATT_COLS, key=ATT_COLS.get)
    w_att = jnp.concatenate([w_in_l[:, src[n][0]:src[n][1]] for n in order], axis=1).astype(BF16)
    g0, g1 = src["b_gate"]
    m0, m1 = src["merge_gate"]
    pad = jnp.zeros((d, LANES - (g1 - g0)), w_in_l.dtype)
    w_gate = jnp.concatenate([w_in_l[:, m0:m1], w_in_l[:, g0:g1], pad], axis=1).astype(BF16)
    return w_att, w_gate


def _compress_rows(att, name):
    bsz, s, _ = att.shape
    c0 = ATT_COLS[name]
    return att[:, :, c0:c0 + LANES].reshape(bsz, s // CMP_STRIDE, CMP_STRIDE * LANES)


def _compress_params(pos, w1):
    hidden = w1.shape[-1]
    pos_rows = jnp.broadcast_to(pos.reshape(2, 2, CMP_STRIDE, 1, HEAD_DIM).astype(F32),
                                (2, 2, CMP_STRIDE, B_KV, HEAD_DIM)).reshape(2, 2, 1, CMP_STRIDE * LANES)
    w1r = w1.reshape(2, 2, CMP_STRIDE, HEAD_DIM, hidden)
    w1g = jnp.einsum('whjdc,gk->whgjkdc', w1r, jnp.eye(B_KV, dtype=w1.dtype))
    return pos_rows, w1g.reshape(2, 2, B_KV, CMP_STRIDE * LANES, hidden).astype(BF16)


def kernel(x, c, rel_bias, ada_w, ada_b, ln_pre, ln_post, ffn_w_gate, ffn_w_up, ffn_w_down, w_in, attn_sinks,
           cmp_pos, cmp_w1, cmp_w2, w_branch, w_out):
    bsz, s, d = x.shape
    depth = ada_w.shape[0]
    assert s % QB == 0 and s // CMP_STRIDE == LANES and CMP_BLOCK == 2 * CMP_STRIDE
    assert d == MERGE_W // N_BRANCH and A_WINDOW == QB and _far_bucket_is_constant()
    n_rows = s // CMP_STRIDE
    n_sel = s // SEL_BLOCK
    hpg_a, hpg_b = A_HEADS // A_KV, B_HEADS // B_KV

    mod = _modulation(c, ada_w, ada_b)
    table_a = rel_bias[:, :A_HEADS]
    table_b = rel_bias[:, A_HEADS:A_HEADS + B_HEADS]
    bias_a = _band_bias(table_a, A_WINDOW, A_KV)
    bias_w = _band_bias(table_b, B_WINDOW, B_KV)
    bias_near = _band_bias(table_b, SLC_TILE, B_KV, mask_band=False, tile=SLC_TILE)
    bias_far = jnp.repeat(table_b[REL_BUCKETS - 1].astype(F32), SLC_TILE).reshape(B_KV, hpg_b * SLC_TILE, 1)
    overlap = _overlap_matrix(n_rows, n_sel)

    for l in range(depth):
        mod_l = mod[l]
        x = _ffn(x, mod_l, ln_pre[l, 0:1], ln_post[l, 0:1], ffn_w_gate[l, 0].astype(BF16),
                 ffn_w_up[l, 0].astype(BF16), ffn_w_down[l, 0].astype(BF16), sub=0)

        w_att, w_gate = _layer_weights(w_in[l], d)
        att, gates = _inproj(x, mod_l, ln_pre[l, 1:2], w_att, w_gate)
        bias_a_l = _with_sink_column(bias_a, attn_sinks[l])
        y_a = _band_attention(att, ATT_COLS["a_q"], ATT_COLS["a_k"], ATT_COLS["a_v"], bias_a_l, A_WINDOW, True)
        pos_rows, w1_groups = _compress_params(cmp_pos[l], cmp_w1[l])
        kc, vc = _compress(_compress_rows(att, "b_k_cmp"), _compress_rows(att, "b_v_cmp"), pos_rows, w1_groups,
                           cmp_w2[l].astype(BF16))
        o_cmp, sel = _cmpsel(att, kc, vc, overlap)
        o_slc = _slc_attention(att, sel, bias_near, bias_far, n_sel)
        o_win = _band_attention(att, ATT_COLS["b_q"], ATT_COLS["b_k_win"], ATT_COLS["b_v_win"], bias_w, B_WINDOW,
                                False)
        y_c = _stick_attention(att)
        x = _merge(x, mod_l, ln_post[l, 1:2], y_a, o_cmp, o_slc, o_win, y_c, gates,
                   w_branch[l].astype(BF16), w_out[l].astype(BF16))

        x = _ffn(x, mod_l, ln_pre[l, 2:3], ln_post[l, 2:3], ffn_w_gate[l, 1].astype(BF16),
                 ffn_w_up[l, 1].astype(BF16), ffn_w_down[l, 1].astype(BF16), sub=2)
    return x
```

```python
import functools
import math

import numpy as np
import jax
import jax.numpy as jnp
from jax import lax
from jax.experimental import pallas as pl
from jax.experimental.pallas import tpu as pltpu

F32 = jnp.float32
BF16 = jnp.bfloat16

HEAD_DIM = 64
A_HEADS, A_KV, A_WINDOW = 8, 2, 128
B_HEADS, B_KV, B_WINDOW = 8, 2, 512
CMP_BLOCK, CMP_STRIDE, CMP_HIDDEN = 32, 16, 128
SEL_BLOCK, SEL_TOPN = 64, 16
C_HEADS = 8
BRANCH_WIDTH = 512
N_BRANCH = 3
REL_BUCKETS, REL_MAX_DIST = 32, 128
N_SUB = 3
EPS = 1e-6
FFN_RES = 0.5
NEG = -1e30

LANES = 128
QB = 128
SLC_TILE = 256
STICK_TILE = 256
VMEM_LIMIT = 56 * 1024 * 1024

ATT_COLS = {
    "a_q": 0, "b_q": 512, "c_q": 1024, "c_k": 1536, "c_v": 2048,
    "a_k": 2560, "a_v": 2688, "b_k_cmp": 2816, "b_v_cmp": 2944,
    "b_k_slc": 3072, "b_v_slc": 3200, "b_k_win": 3328, "b_v_win": 3456,
}
ATT_W = 3584
MERGE_W = 3072
GATE_W = MERGE_W + LANES


def _src_layout(d_model):
    widths = [
        ("a_q", A_HEADS * HEAD_DIM), ("a_k", A_KV * HEAD_DIM), ("a_v", A_KV * HEAD_DIM),
        ("b_q", B_HEADS * HEAD_DIM),
        ("b_k_cmp", B_KV * HEAD_DIM), ("b_v_cmp", B_KV * HEAD_DIM),
        ("b_k_slc", B_KV * HEAD_DIM), ("b_v_slc", B_KV * HEAD_DIM),
        ("b_k_win", B_KV * HEAD_DIM), ("b_v_win", B_KV * HEAD_DIM),
        ("b_gate", 3 * B_HEADS),
        ("c_q", C_HEADS * HEAD_DIM), ("c_k", C_HEADS * HEAD_DIM), ("c_v", C_HEADS * HEAD_DIM),
        ("merge_gate", N_BRANCH * d_model),
    ]
    out, off = {}, 0
    for name, w in widths:
        out[name] = (off, off + w)
        off += w
    return out


def _cparams(sem):
    return pltpu.CompilerParams(dimension_semantics=sem, vmem_limit_bytes=VMEM_LIMIT)


def _resident(shape, index_map):
    return pl.BlockSpec(shape, index_map, pipeline_mode=pl.Buffered(1))


def _rms(x, gain):
    return x * lax.rsqrt(jnp.mean(x * x, axis=-1, keepdims=True) + EPS) * gain


def _modulated(x, mod_ref, sub, gain):
    shift = mod_ref[0, 3 * sub:3 * sub + 1, :]
    scale = mod_ref[0, 3 * sub + 1:3 * sub + 2, :]
    return _rms(x, gain) * (1.0 + scale) + shift


def _dot(a, b):
    return jnp.dot(a, b, preferred_element_type=F32)


def _dot_nt(a, b):
    return lax.dot_general(a, b, (((1,), (1,)), ((), ())), preferred_element_type=F32)


def _split3(x):
    hi = x.astype(BF16)
    r1 = x - hi.astype(F32)
    mid = r1.astype(BF16)
    lo = (r1 - mid.astype(F32)).astype(BF16)
    return hi, mid, lo


def _lane_half(shape):
    return (lax.broadcasted_iota(jnp.int32, shape, 1) % LANES) // HEAD_DIM


def _stack_group_queries(q, group, heads_per_group):
    half = _lane_half((q.shape[0], LANES))
    parts = []
    for hp in range(heads_per_group):
        h = group * heads_per_group + hp
        tile = q[:, (h // 2) * LANES:(h // 2 + 1) * LANES]
        if h % 2 != group:
            tile = pltpu.roll(tile, HEAD_DIM, 1)
        parts.append(jnp.where(half == group, tile, 0.0))
    scale = 1.0 / math.sqrt(HEAD_DIM)
    return (jnp.concatenate(parts, axis=0) * scale).astype(BF16)


def _unstack_group_outputs(pv, group, heads_per_group):
    tile = pv.shape[0] // heads_per_group
    half = _lane_half((tile, LANES))
    tiles = []
    for pair in range(heads_per_group // 2):
        even = pv[(2 * pair) * tile:(2 * pair + 1) * tile]
        odd = pv[(2 * pair + 1) * tile:(2 * pair + 2) * tile]
        if group != 0:
            even = pltpu.roll(even, HEAD_DIM, 1)
        if group != 1:
            odd = pltpu.roll(odd, HEAD_DIM, 1)
        tiles.append(jnp.where(half == 0, even, odd))
    return jnp.concatenate(tiles, axis=1)


def _mod_kernel(c_ref, w_ref, b_ref, o_ref):
    c = c_ref[...]
    act = (c * jax.nn.sigmoid(c)).astype(BF16)
    o_ref[0] = _dot(act, w_ref[0].astype(BF16)) + b_ref[0]


def _modulation(c, ada_w, ada_b):
    depth, d, n = ada_w.shape
    bsz = c.shape[0]
    tn = n // 4
    out = pl.pallas_call(
        _mod_kernel,
        grid=(depth, n // tn),
        in_specs=[
            pl.BlockSpec((bsz, d), lambda l, j: (0, 0)),
            pl.BlockSpec((1, d, tn), lambda l, j: (l, 0, j)),
            pl.BlockSpec((1, 1, tn), lambda l, j: (l, 0, j)),
        ],
        out_specs=pl.BlockSpec((1, bsz, tn), lambda l, j: (l, 0, j)),
        out_shape=jax.ShapeDtypeStruct((depth, bsz, n), F32),
        compiler_params=_cparams(("arbitrary", "arbitrary")),
        name="adaln_mod",
    )(c, ada_w, ada_b.reshape(depth, 1, n))
    return out.reshape(depth, bsz, N_SUB * 3, d)


def _ffn_kernel(x_ref, mod_ref, lnpre_ref, lnpost_ref, wg_ref, wu_ref, wd_ref, o_ref, acc_ref, *, sub, tf):
    x = x_ref[0]
    h = _modulated(x, mod_ref, sub, lnpre_ref[...]).astype(BF16)
    d_ff = wg_ref.shape[1]
    for j in range(d_ff // tf):
        g = _dot(h, wg_ref[:, j * tf:(j + 1) * tf])
        u = _dot(h, wu_ref[:, j * tf:(j + 1) * tf])
        a = (g * jax.nn.sigmoid(g) * u).astype(BF16)
        part = _dot(a, wd_ref[j * tf:(j + 1) * tf, :])
        if j == 0:
            acc_ref[...] = part
        else:
            acc_ref[...] += part
    gate = mod_ref[0, 3 * sub + 2:3 * sub + 3, :]
    o_ref[0] = x + FFN_RES * gate * _rms(acc_ref[...], lnpost_ref[...])


def _ffn(x, mod, ln_pre, ln_post, wg, wu, wd, sub, tm=512, tf=256):
    bsz, s, d = x.shape
    d_ff = wg.shape[1]
    row = lambda b, i: (b, i, 0)
    const2 = lambda b, i: (0, 0)
    return pl.pallas_call(
        functools.partial(_ffn_kernel, sub=sub, tf=tf),
        grid=(bsz, s // tm),
        in_specs=[
            pl.BlockSpec((1, tm, d), row),
            pl.BlockSpec((1, N_SUB * 3, d), lambda b, i: (b, 0, 0)),
            pl.BlockSpec((1, d), const2),
            pl.BlockSpec((1, d), const2),
            _resident((d, d_ff), const2),
            _resident((d, d_ff), const2),
            _resident((d_ff, d), const2),
        ],
        out_specs=pl.BlockSpec((1, tm, d), row),
        out_shape=jax.ShapeDtypeStruct(x.shape, F32),
        scratch_shapes=[pltpu.VMEM((tm, d), F32)],
        compiler_params=_cparams(("arbitrary", "arbitrary")),
        name="ffn",
    )(x, mod, ln_pre, ln_post, wg, wu, wd)


def _inproj_kernel(x_ref, mod_ref, lnpre_ref, watt_ref, wgate_ref, att_ref, gate_ref, *, att_chunk, gate_chunk):
    h = _modulated(x_ref[0], mod_ref, 1, lnpre_ref[...]).astype(BF16)
    for c0 in range(0, ATT_W, att_chunk):
        att_ref[0, :, c0:c0 + att_chunk] = _dot(h, watt_ref[:, c0:c0 + att_chunk]).astype(BF16)
    for c0 in range(0, GATE_W, gate_chunk):
        gate_ref[0, :, c0:c0 + gate_chunk] = _dot(h, wgate_ref[:, c0:c0 + gate_chunk]).astype(BF16)


def _inproj(x, mod, ln_pre, w_att, w_gate, tm=512):
    bsz, s, d = x.shape
    row = lambda b, i: (b, i, 0)
    const2 = lambda b, i: (0, 0)
    return pl.pallas_call(
        functools.partial(_inproj_kernel, att_chunk=512, gate_chunk=640),
        grid=(bsz, s // tm),
        in_specs=[
            pl.BlockSpec((1, tm, d), row),
            pl.BlockSpec((1, N_SUB * 3, d), lambda b, i: (b, 0, 0)),
            pl.BlockSpec((1, d), const2),
            _resident((d, ATT_W), const2),
            _resident((d, GATE_W), const2),
        ],
        out_specs=[pl.BlockSpec((1, tm, ATT_W), row), pl.BlockSpec((1, tm, GATE_W), row)],
        out_shape=[jax.ShapeDtypeStruct((bsz, s, ATT_W), BF16), jax.ShapeDtypeStruct((bsz, s, GATE_W), BF16)],
        compiler_params=_cparams(("arbitrary", "arbitrary")),
        name="inproj",
    )(x, mod, ln_pre, w_att, w_gate)


def _band_kernel(q_ref, k_ref, v_ref, bias_ref, o_ref, *, window, has_sink, q_tiles):
    nkb = window // QB + 1
    span = nkb * QB
    n_groups = bias_ref.shape[0]
    hpg = bias_ref.shape[1] // QB
    jk = lax.broadcasted_iota(jnp.int32, (1, span), 1)
    row0 = lax.broadcasted_iota(jnp.int32, (span, LANES), 0) == 0
    vhalf = _lane_half((span, LANES))
    chains = [(u, g) for u in range(q_tiles) for g in range(n_groups)]
    vspans, in_seqs, scores = [], [], []
    for u in range(q_tiles):
        i = pl.program_id(1) * q_tiles + u
        starts = [jnp.maximum(i - (nkb - 1) + kb, 0) * QB for kb in range(nkb)]
        kspan = jnp.concatenate([k_ref[0, pl.ds(pl.multiple_of(st, QB), QB), :] for st in starts], axis=0)
        vspan = jnp.concatenate([v_ref[0, pl.ds(pl.multiple_of(st, QB), QB), :] for st in starts], axis=0)
        in_seq = jk >= (nkb - 1 - i) * QB
        if has_sink:
            kspan = jnp.where(row0, jnp.zeros_like(kspan), kspan)
            vspan = jnp.where(row0, jnp.zeros_like(vspan), vspan)
            in_seq = in_seq | (jk == 0)
        vspans.append(vspan)
        in_seqs.append(in_seq)
        q = q_ref[0, u * QB:(u + 1) * QB, :].astype(F32)
        for g in range(n_groups):
            scores.append(_dot_nt(_stack_group_queries(q, g, hpg), kspan))
    weights = []
    for (u, g), z in zip(chains, scores):
        logits = jnp.where(in_seqs[u], z + bias_ref[g], NEG)
        weights.append(jnp.exp(logits - jnp.max(logits, axis=-1, keepdims=True)).astype(BF16))
    pvs = [_dot(e, jnp.where(vhalf == g, vspans[u], jnp.ones_like(vspans[u]))) for (u, g), e in zip(chains, weights)]
    for u in range(q_tiles):
        outs = []
        for g in range(n_groups):
            pv = pvs[u * n_groups + g]
            outs.append(_unstack_group_outputs(pv / pltpu.roll(pv, HEAD_DIM, 1), g, hpg))
        o_ref[0, u * QB:(u + 1) * QB, :] = jnp.concatenate(outs, axis=1).astype(BF16)


def _band_attention(att, q_col, k_col, v_col, bias, window, has_sink, q_tiles=4):
    bsz, s, _ = att.shape
    n_groups, rows, span = bias.shape
    width = n_groups * (rows // QB) * HEAD_DIM
    tq = q_tiles * QB
    return pl.pallas_call(
        functools.partial(_band_kernel, window=window, has_sink=has_sink, q_tiles=q_tiles),
        grid=(bsz, s // tq),
        in_specs=[
            pl.BlockSpec((1, tq, width), lambda b, i: (b, i, q_col // width)),
            pl.BlockSpec((1, s, LANES), lambda b, i: (b, 0, k_col // LANES)),
            pl.BlockSpec((1, s, LANES), lambda b, i: (b, 0, v_col // LANES)),
            _resident((n_groups, rows, span), lambda b, i: (0, 0, 0)),
        ],
        out_specs=pl.BlockSpec((1, tq, width), lambda b, i: (b, i, 0)),
        out_shape=jax.ShapeDtypeStruct((bsz, s, width), BF16),
        compiler_params=_cparams(("arbitrary", "arbitrary")),
        name=f"band{window}",
    )(att, att, att, bias)


def _compress_kernel(rk_ref, rv_ref, pos_ref, w1_ref, w2_ref, kc_ref, vc_ref):
    n_rows = rk_ref.shape[1]
    last = lax.broadcasted_iota(jnp.int32, (n_rows, 1), 0) == n_rows - 1
    for which, (r_ref, o_ref) in enumerate(((rk_ref, kc_ref), (rv_ref, vc_ref))):
        r = r_ref[0].astype(F32)
        lo = (r + pos_ref[which, 0]).astype(BF16)
        hi = (r + pos_ref[which, 1]).astype(BF16)
        halves = []
        for g in range(B_KV):
            hid = _dot(lo, w1_ref[which, 0, g]) + pltpu.roll(_dot(hi, w1_ref[which, 1, g]), n_rows - 1, 0)
            out = _dot(jax.nn.gelu(hid).astype(BF16), w2_ref[which])
            halves.append(jnp.where(last, 0.0, out))
        o_ref[0] = jnp.concatenate(halves, axis=1).astype(BF16)


def _compress(rk, rv, pos, w1, w2):
    bsz, n_rows, width = rk.shape
    g = B_KV
    r_spec = pl.BlockSpec((1, n_rows, width), lambda b: (b, 0, 0))
    o_spec = pl.BlockSpec((1, n_rows, g * HEAD_DIM), lambda b: (b, 0, 0))
    o_shape = jax.ShapeDtypeStruct((bsz, n_rows, g * HEAD_DIM), BF16)
    return pl.pallas_call(
        _compress_kernel,
        grid=(bsz,),
        in_specs=[
            r_spec, r_spec,
            pl.BlockSpec(pos.shape, lambda b: (0, 0, 0, 0)),
            pl.BlockSpec(w1.shape, lambda b: (0, 0, 0, 0, 0)),
            pl.BlockSpec(w2.shape, lambda b: (0, 0, 0)),
        ],
        out_specs=[o_spec, o_spec],
        out_shape=[o_shape, o_shape],
        compiler_params=_cparams(("arbitrary",)),
        name="nsa_compress",
    )(rk, rv, pos, w1, w2)


def _cmpsel_kernel(q_ref, kc_ref, vc_ref, ov_ref, o_ref, sel_ref):
    i = pl.program_id(1)
    q = q_ref[0].astype(F32)
    kc = kc_ref[0]
    vc = vc_ref[0]
    n_cmp_rows = kc.shape[0]
    hpg = B_HEADS // B_KV
    rows = hpg * QB
    t_stack = i * QB + lax.broadcasted_iota(jnp.int32, (rows, 1), 0) % QB
    n_idx = lax.broadcasted_iota(jnp.int32, (1, n_cmp_rows), 1)
    cmp_ok = (n_idx * CMP_STRIDE + CMP_BLOCK - 1 <= t_stack) & (n_idx < n_cmp_rows - 1)
    n_sel = ov_ref.shape[0]
    blk = lax.broadcasted_iota(jnp.int32, (n_sel, QB), 0)
    t = i * QB + lax.broadcasted_iota(jnp.int32, (n_sel, QB), 1)
    cur = t // SEL_BLOCK
    forced = (blk == 0) | (blk == cur) | (blk == cur - 1)
    future = blk * SEL_BLOCK > t
    ov = ov_ref[...]
    scores = [_dot_nt(_stack_group_queries(q, g, hpg), kc) for g in range(B_KV)]
    probs = []
    for z in scores:
        logits = jnp.where(cmp_ok, z, NEG)
        m = jnp.max(logits, axis=-1, keepdims=True)
        e = jnp.where(cmp_ok, jnp.exp(logits - m), 0.0)
        ssum = jnp.sum(e, axis=-1, keepdims=True)
        probs.append(e / jnp.where(ssum > 0, ssum, 1.0))
    outs = [_unstack_group_outputs(_dot(p.astype(BF16), vc), g, hpg) for g, p in enumerate(probs)]
    imps = []
    for p in probs:
        p_heads = p[0:QB]
        for hp in range(1, hpg):
            p_heads = p_heads + p[hp * QB:(hp + 1) * QB]
        hi, mid, lo = _split3(p_heads)
        imps.append(_dot_nt(ov, hi) + _dot_nt(ov, mid) + _dot_nt(ov, lo))
    sel_t = []
    for imp in imps:
        prio = jnp.where(forced, jnp.inf, jnp.where(future, -jnp.inf, imp))
        rank = jnp.zeros((n_sel, QB), F32)
        for c in range(n_sel):
            cand = prio[c:c + 1, :]
            rank = rank + jnp.where(blk > c, jnp.where(cand >= prio, 1.0, 0.0), jnp.where(cand > prio, 1.0, 0.0))
        sel_t.append(jnp.where(rank < min(SEL_TOPN, n_sel), 1.0, 0.0))
    sel_t.append(jnp.zeros((LANES - B_KV * n_sel, QB), F32))
    o_ref[0] = jnp.concatenate(outs, axis=1).astype(BF16)
    sel_ref[0] = jnp.concatenate(sel_t, axis=0).T.astype(BF16)


def _cmpsel(att, kc, vc, overlap):
    bsz, s, _ = att.shape
    n_rows = kc.shape[1]
    width = B_HEADS * HEAD_DIM
    return pl.pallas_call(
        _cmpsel_kernel,
        grid=(bsz, s // QB),
        in_specs=[
            pl.BlockSpec((1, QB, width), lambda b, i: (b, i, ATT_COLS["b_q"] // width)),
            pl.BlockSpec((1, n_rows, LANES), lambda b, i: (b, 0, 0)),
            pl.BlockSpec((1, n_rows, LANES), lambda b, i: (b, 0, 0)),
            pl.BlockSpec(overlap.shape, lambda b, i: (0, 0)),
        ],
        out_specs=[pl.BlockSpec((1, QB, width), lambda b, i: (b, i, 0)),
                   pl.BlockSpec((1, QB, LANES), lambda b, i: (b, i, 0))],
        out_shape=[jax.ShapeDtypeStruct((bsz, s, width), BF16), jax.ShapeDtypeStruct((bsz, s, LANES), BF16)],
        compiler_params=_cparams(("arbitrary", "arbitrary")),
        name="nsa_cmpsel",
    )(att, kc, vc, overlap)


def _slc_kernel(q_ref, k_ref, v_ref, sel_ref, bias_ref, far_ref, o_ref, *, n_sel):
    i = pl.program_id(1)
    hpg = B_HEADS // B_KV
    tile = sel_ref.shape[1]
    rows = hpg * tile
    q = q_ref[0].astype(F32)
    sel = sel_ref[0]
    sel_row = lax.broadcasted_iota(jnp.int32, (LANES, tile), 0)
    key_blk = lax.broadcasted_iota(jnp.int32, (LANES, tile), 1) // SEL_BLOCK
    iq = lax.broadcasted_iota(jnp.int32, (tile, tile), 0)
    jk = lax.broadcasted_iota(jnp.int32, (tile, tile), 1)
    vhalf = _lane_half((tile, LANES))
    qs = [_stack_group_queries(q, g, hpg) for g in range(B_KV)]
    pair_rows = 2 * tile
    chains = [(g, r0) for g in range(B_KV) for r0 in range(0, rows, pair_rows)]

    def tile_step(j, carry, near, causal):
        st = pl.multiple_of(j * tile, tile)
        kt = k_ref[0, pl.ds(st, tile), :]
        vt = v_ref[0, pl.ds(st, tile), :]
        picked, zs, es, m_news, new = {}, {}, {}, {}, {}

        def stage(s, c):
            g, r0 = chains[c]
            if s == 0:
                if g not in picked:
                    expand = jnp.where(sel_row == g * n_sel + j * (tile // SEL_BLOCK) + key_blk, 1.0, 0.0)
                    ok = _dot(sel, expand.astype(BF16)) > 0.5
                    picked[g] = ok if causal is None else ok & causal
                zs[c] = _dot_nt(qs[g][r0:r0 + pair_rows], kt)
            elif s == 1:
                if near is None:
                    z = zs[c] + far_ref[g, r0:r0 + pair_rows]
                else:
                    z = zs[c] + bias_ref[g, r0:r0 + pair_rows, near * tile:(near + 1) * tile]
                logits = jnp.concatenate(
                    [jnp.where(picked[g], z[h * tile:(h + 1) * tile], NEG) for h in range(2)], axis=0)
                m_news[c] = jnp.maximum(carry[c][0], jnp.max(logits, axis=-1, keepdims=True))
                es[c] = jnp.exp(logits - m_news[c]).astype(BF16)
            else:
                m, acc = carry[c]
                pv = _dot(es[c], jnp.where(vhalf == g, vt, jnp.ones_like(vt)))
                new[c] = (m_news[c], jnp.exp(m - m_news[c]) * acc + pv)

        n_stages = 3
        for t in range(len(chains) + n_stages - 1):
            for s in range(n_stages):
                if 0 <= t - s < len(chains):
                    stage(s, t - s)
        return tuple(new[c] for c in range(len(chains)))

    init = tuple((jnp.full((pair_rows, 1), NEG, F32), jnp.zeros((pair_rows, LANES), F32)) for _ in chains)
    carry = tile_step(i, init, 1, jk <= iq)
    carry = tile_step(jnp.maximum(i - 1, 0), carry, 0, (i >= 1) & (jk >= 0))
    carry = lax.fori_loop(0, jnp.maximum(i - 1, 0), lambda j, c: tile_step(j, c, None, None), carry)
    outs = []
    for g in range(B_KV):
        acc = jnp.concatenate([carry[c][1] for c, (cg, _) in enumerate(chains) if cg == g], axis=0)
        outs.append(_unstack_group_outputs(acc / pltpu.roll(acc, HEAD_DIM, 1), g, hpg))
    o_ref[0] = jnp.concatenate(outs, axis=1).astype(BF16)


def _slc_attention(att, sel, bias_near, bias_far, n_sel):
    bsz, s, _ = att.shape
    width = B_HEADS * HEAD_DIM
    n_groups, rows, span = bias_near.shape
    tile = span // 2
    return pl.pallas_call(
        functools.partial(_slc_kernel, n_sel=n_sel),
        grid=(bsz, s // tile),
        in_specs=[
            pl.BlockSpec((1, tile, width), lambda b, i: (b, i, ATT_COLS["b_q"] // width)),
            pl.BlockSpec((1, s, LANES), lambda b, i: (b, 0, ATT_COLS["b_k_slc"] // LANES)),
            pl.BlockSpec((1, s, LANES), lambda b, i: (b, 0, ATT_COLS["b_v_slc"] // LANES)),
            pl.BlockSpec((1, tile, LANES), lambda b, i: (b, i, 0)),
            _resident((n_groups, rows, span), lambda b, i: (0, 0, 0)),
            pl.BlockSpec((n_groups, rows, 1), lambda b, i: (0, 0, 0)),
        ],
        out_specs=pl.BlockSpec((1, tile, width), lambda b, i: (b, i, 0)),
        out_shape=jax.ShapeDtypeStruct((bsz, s, width), BF16),
        compiler_params=_cparams(("arbitrary", "arbitrary")),
        name="nsa_slc",
    )(att, att, att, sel, bias_near, bias_far)


def _stick_kernel(q_ref, k_ref, v_ref, o_ref):
    i = pl.program_id(1)
    pairs = C_HEADS // 2
    tile = q_ref.shape[1]
    rows = 2 * tile
    half = _lane_half((tile, LANES))
    strict = (lax.broadcasted_iota(jnp.int32, (rows, tile), 1)
              < lax.broadcasted_iota(jnp.int32, (rows, tile), 0) % tile)
    later_than = jnp.where(lax.broadcasted_iota(jnp.int32, (2 * tile, tile), 0) % tile
                           > lax.broadcasted_iota(jnp.int32, (2 * tile, tile), 1), 1.0, 0.0).astype(BF16)
    qs = []
    for p in range(pairs):
        q = q_ref[0, :, p * LANES:(p + 1) * LANES].astype(F32) * (1.0 / math.sqrt(HEAD_DIM))
        qs.append(jnp.concatenate([jnp.where(half == 0, q, 0.0), jnp.where(half == 1, q, 0.0)],
                                  axis=0).astype(BF16))

    def tile_step(j, carry, mask):
        st = pl.multiple_of(j * tile, tile)
        zs, log_beta, log_keep, later, ws, new = {}, {}, {}, {}, {}, {}

        def stage(s, p):
            if s == 0:
                zs[p] = _dot_nt(qs[p], k_ref[0, pl.ds(st, tile), p * LANES:(p + 1) * LANES])
            elif s == 1:
                z = zs[p]
                neg_abs = pltpu.bitcast(pltpu.bitcast(z, jnp.uint32) | jnp.uint32(0x80000000), F32)
                lb = jnp.minimum(z, 0.0) - jnp.log(1.0 + jnp.exp(neg_abs))
                lk = lb - z
                if mask is not None:
                    lk = jnp.where(mask, lk, 0.0)
                log_beta[p], log_keep[p] = lb, lk
            elif s == 2:
                lk = log_keep[p]
                hi = lk.astype(BF16)
                later[p] = _dot(jnp.concatenate([hi, (lk - hi.astype(F32)).astype(BF16)], axis=1), later_than)
            elif s == 3:
                w = jnp.exp(log_beta[p] + later[p] + carry[p][0])
                if mask is not None:
                    w = jnp.where(mask, w, 0.0)
                ws[p] = w.astype(BF16)
            else:
                tail, acc = carry[p]
                vt = v_ref[0, pl.ds(st, tile), p * LANES:(p + 1) * LANES]
                new[p] = (tail + jnp.sum(log_keep[p], axis=-1, keepdims=True), acc + _dot(ws[p], vt))

        n_stages = 5
        for t in range(pairs + n_stages - 1):
            for s in range(n_stages):
                if 0 <= t - s < pairs:
                    stage(s, t - s)
        return tuple(new[p] for p in range(pairs))

    init = tuple((jnp.zeros((rows, 1), F32), jnp.zeros((rows, LANES), F32)) for _ in range(pairs))
    carry = tile_step(i, init, strict)
    carry = lax.fori_loop(0, i, lambda step, c: tile_step(i - 1 - step, c, None), carry)
    o_ref[0] = jnp.concatenate([jnp.where(half == 0, acc[0:tile], acc[tile:rows]) for _, acc in carry],
                               axis=1).astype(BF16)


def _stick_attention(att, tile=STICK_TILE):
    bsz, s, _ = att.shape
    width = C_HEADS * HEAD_DIM
    return pl.pallas_call(
        _stick_kernel,
        grid=(bsz, s // tile),
        in_specs=[
            pl.BlockSpec((1, tile, width), lambda b, i: (b, i, ATT_COLS["c_q"] // width)),
            pl.BlockSpec((1, s, width), lambda b, i: (b, 0, ATT_COLS["c_k"] // width)),
            pl.BlockSpec((1, s, width), lambda b, i: (b, 0, ATT_COLS["c_v"] // width)),
        ],
        out_specs=pl.BlockSpec((1, tile, width), lambda b, i: (b, i, 0)),
        out_shape=jax.ShapeDtypeStruct((bsz, s, width), BF16),
        compiler_params=_cparams(("arbitrary", "arbitrary")),
        name="stick",
    )(att, att, att)


def _merge_kernel(x_ref, mod_ref, lnpost_ref, ya_ref, oc_ref, os_ref, ow_ref, yc_ref, mg_ref, bg_ref,
                  wb_ref, wo_ref, o_ref):
    d = x_ref.shape[2]
    bw = ya_ref.shape[2]
    head_gate = jax.nn.sigmoid(bg_ref[0].astype(F32))
    gate_hi = head_gate.astype(BF16)
    gate_lo = (head_gate - gate_hi.astype(F32)).astype(BF16)
    gate_row = lax.broadcasted_iota(jnp.int32, (LANES, bw), 0)
    head_of_col = lax.broadcasted_iota(jnp.int32, (LANES, bw), 1) // HEAD_DIM
    y_b = jnp.zeros(ya_ref.shape[1:], F32)
    for br, ref in enumerate((oc_ref, os_ref, ow_ref)):
        expand = jnp.where(gate_row == br * B_HEADS + head_of_col, 1.0, 0.0).astype(BF16)
        y_b = y_b + (_dot(gate_hi, expand) + _dot(gate_lo, expand)) * ref[0].astype(F32)
    merged = jnp.zeros((x_ref.shape[1], d), F32)
    for br, y in enumerate((ya_ref[0], y_b.astype(BF16), yc_ref[0])):
        merged = merged + jax.nn.sigmoid(mg_ref[0, :, br * d:(br + 1) * d].astype(F32)) * _dot(y, wb_ref[br])
    t = _dot(merged.astype(BF16), wo_ref[...])
    gate = mod_ref[0, 5:6, :]
    o_ref[0] = x_ref[0] + gate * _rms(t, lnpost_ref[...])


def _merge(x, mod, ln_post, y_a, o_cmp, o_slc, o_win, y_c, gates, w_branch, w_out, tm=256):
    bsz, s, d = x.shape
    bw = y_a.shape[2]
    row = lambda b, i: (b, i, 0)
    const2 = lambda b, i: (0, 0)
    branch = pl.BlockSpec((1, tm, bw), row)
    return pl.pallas_call(
        _merge_kernel,
        grid=(bsz, s // tm),
        in_specs=[
            pl.BlockSpec((1, tm, d), row),
            pl.BlockSpec((1, N_SUB * 3, d), lambda b, i: (b, 0, 0)),
            pl.BlockSpec((1, d), const2),
            branch, branch, branch, branch, branch,
            pl.BlockSpec((1, tm, MERGE_W), row),
            pl.BlockSpec((1, tm, LANES), lambda b, i: (b, i, MERGE_W // LANES)),
            _resident(w_branch.shape, lambda b, i: (0, 0, 0)),
            _resident(w_out.shape, const2),
        ],
        out_specs=pl.BlockSpec((1, tm, d), row),
        out_shape=jax.ShapeDtypeStruct(x.shape, F32),
        compiler_params=_cparams(("arbitrary", "arbitrary")),
        name="merge",
    )(x, mod, ln_post, y_a, o_cmp, o_slc, o_win, y_c, gates, gates, w_branch, w_out)


def _t5_bucket(dist):
    max_exact = REL_BUCKETS // 2
    d = jnp.maximum(dist, 0)
    ratio = jnp.log(jnp.maximum(d, 1).astype(F32) / max_exact) / math.log(REL_MAX_DIST / max_exact)
    large = jnp.minimum(max_exact + (ratio * (REL_BUCKETS - max_exact)).astype(jnp.int32), REL_BUCKETS - 1)
    return jnp.where(d < max_exact, d, large)


def _band_bias(table, window, n_groups, mask_band=True, tile=QB):
    span = window + tile
    dist = window + jnp.arange(tile)[:, None] - jnp.arange(span)[None, :]
    onehot = (_t5_bucket(dist)[..., None] == jnp.arange(REL_BUCKETS)).astype(F32)
    bias = jnp.einsum('qkb,bh->hqk', onehot, table.astype(F32), precision=lax.Precision.HIGHEST)
    if mask_band:
        bias = jnp.where(((dist >= 0) & (dist < window))[None], bias, NEG)
    return bias.reshape(n_groups, -1, span)


def _with_sink_column(bias, sinks):
    n_groups, rows, _ = bias.shape
    col = jnp.repeat(sinks.astype(F32), QB).reshape(n_groups, rows)
    return bias.at[:, :, 0].set(col)


def _far_bucket_is_constant():
    d = np.arange(QB + 1, 1 << 16).astype(np.float32)
    max_exact = REL_BUCKETS // 2
    ratio = np.log(d / max_exact) / math.log(REL_MAX_DIST / max_exact)
    return bool(np.all(max_exact + (ratio * (REL_BUCKETS - max_exact)).astype(np.int32) >= REL_BUCKETS - 1))


def _overlap_matrix(n_rows, n_sel):
    cstart = np.arange(n_rows)[None, :] * CMP_STRIDE
    sstart = np.arange(n_sel)[:, None] * SEL_BLOCK
    ov = (cstart < sstart + SEL_BLOCK) & (cstart + CMP_BLOCK > sstart) & (np.arange(n_rows)[None, :] < n_rows - 1)
    return jnp.asarray(ov, dtype=BF16)


def _layer_weights(w_in_l, d):
    src = _src_layout(d)
    order = sorted(ATT_COLS, key=ATT_COLS.get)
    w_att = jnp.concatenate([w_in_l[:, src[n][0]:src[n][1]] for n in order], axis=1).astype(BF16)
    g0, g1 = src["b_gate"]
    m0, m1 = src["merge_gate"]
    pad = jnp.zeros((d, LANES - (g1 - g0)), w_in_l.dtype)
    w_gate = jnp.concatenate([w_in_l[:, m0:m1], w_in_l[:, g0:g1], pad], axis=1).astype(BF16)
    return w_att, w_gate


def _compress_rows(att, name):
    bsz, s, _ = att.shape
    c0 = ATT_COLS[name]
    return att[:, :, c0:c0 + LANES].reshape(bsz, s // CMP_STRIDE, CMP_STRIDE * LANES)


def _compress_params(pos, w1):
    hidden = w1.shape[-1]
    pos_rows = jnp.broadcast_to(pos.reshape(2, 2, CMP_STRIDE, 1, HEAD_DIM).astype(F32),
                                (2, 2, CMP_STRIDE, B_KV, HEAD_DIM)).reshape(2, 2, 1, CMP_STRIDE * LANES)
    w1r = w1.reshape(2, 2, CMP_STRIDE, HEAD_DIM, hidden)
    w1g = jnp.einsum('whjdc,gk->whgjkdc', w1r, jnp.eye(B_KV, dtype=w1.dtype))
    return pos_rows, w1g.reshape(2, 2, B_KV, CMP_STRIDE * LANES, hidden).astype(BF16)


def kernel(x, c, rel_bias, ada_w, ada_b, ln_pre, ln_post, ffn_w_gate, ffn_w_up, ffn_w_down, w_in, attn_sinks,
           cmp_pos, cmp_w1, cmp_w2, w_branch, w_out):
    bsz, s, d = x.shape
    depth = ada_w.shape[0]
    assert s % QB == 0 and s // CMP_STRIDE == LANES and CMP_BLOCK == 2 * CMP_STRIDE
    assert d == MERGE_W // N_BRANCH and A_WINDOW == QB and _far_bucket_is_constant()
    n_rows = s // CMP_STRIDE
    n_sel = s // SEL_BLOCK
    hpg_a, hpg_b = A_HEADS // A_KV, B_HEADS // B_KV

    mod = _modulation(c, ada_w, ada_b)
    table_a = rel_bias[:, :A_HEADS]
    table_b = rel_bias[:, A_HEADS:A_HEADS + B_HEADS]
    bias_a = _band_bias(table_a, A_WINDOW, A_KV)
    bias_w = _band_bias(table_b, B_WINDOW, B_KV)
    bias_near = _band_bias(table_b, SLC_TILE, B_KV, mask_band=False, tile=SLC_TILE)
    bias_far = jnp.repeat(table_b[REL_BUCKETS - 1].astype(F32), SLC_TILE).reshape(B_KV, hpg_b * SLC_TILE, 1)
    overlap = _overlap_matrix(n_rows, n_sel)

    for l in range(depth):
        mod_l = mod[l]
        x = _ffn(x, mod_l, ln_pre[l, 0:1], ln_post[l, 0:1], ffn_w_gate[l, 0].astype(BF16),
                 ffn_w_up[l, 0].astype(BF16), ffn_w_down[l, 0].astype(BF16), sub=0)

        w_att, w_gate = _layer_weights(w_in[l], d)
        att, gates = _inproj(x, mod_l, ln_pre[l, 1:2], w_att, w_gate)
        bias_a_l = _with_sink_column(bias_a, attn_sinks[l])
        y_a = _band_attention(att, ATT_COLS["a_q"], ATT_COLS["a_k"], ATT_COLS["a_v"], bias_a_l, A_WINDOW, True)
        pos_rows, w1_groups = _compress_params(cmp_pos[l], cmp_w1[l])
        kc, vc = _compress(_compress_rows(att, "b_k_cmp"), _compress_rows(att, "b_v_cmp"), pos_rows, w1_groups,
                           cmp_w2[l].astype(BF16))
        o_cmp, sel = _cmpsel(att, kc, vc, overlap)
        o_slc = _slc_attention(att, sel, bias_near, bias_far, n_sel)
        o_win = _band_attention(att, ATT_COLS["b_q"], ATT_COLS["b_k_win"], ATT_COLS["b_v_win"], bias_w, B_WINDOW,
                                False)
        y_c = _stick_attention(att)
        x = _merge(x, mod_l, ln_post[l, 1:2], y_a, o_cmp, o_slc, o_win, y_c, gates,
                   w_branch[l].astype(BF16), w_out[l].astype(BF16))

        x = _ffn(x, mod_l, ln_pre[l, 2:3], ln_post[l, 2:3], ffn_w_gate[l, 1].astype(BF16),
                 ffn_w_up[l, 1].astype(BF16), ffn_w_down[l, 1].astype(BF16), sub=2)
    return x
```

```python
import functools
import math

import numpy as np
import jax
import jax.numpy as jnp
from jax import lax
from jax.experimental import pallas as pl
from jax.experimental.pallas import tpu as pltpu

F32 = jnp.float32
BF16 = jnp.bfloat16

HEAD_DIM = 64
A_HEADS, A_KV, A_WINDOW = 8, 2, 128
B_HEADS, B_KV, B_WINDOW = 8, 2, 512
CMP_BLOCK, CMP_STRIDE, CMP_HIDDEN = 32, 16, 128
SEL_BLOCK, SEL_TOPN = 64, 16
C_HEADS = 8
BRANCH_WIDTH = 512
N_BRANCH = 3
REL_BUCKETS, REL_MAX_DIST = 32, 128
N_SUB = 3
EPS = 1e-6
FFN_RES = 0.5
NEG = -1e30

LANES = 128
QB = 128
SLC_TILE = 256
STICK_TILE = 256
VMEM_LIMIT = 56 * 1024 * 1024

ATT_COLS = {
    "a_q": 0, "b_q": 512, "c_q": 1024, "c_k": 1536, "c_v": 2048,
    "a_k": 2560, "a_v": 2688, "b_k_cmp": 2816, "b_v_cmp": 2944,
    "b_k_slc": 3072, "b_v_slc": 3200, "b_k_win": 3328, "b_v_win": 3456,
}
ATT_W = 3584
MERGE_W = 3072
GATE_W = MERGE_W + LANES


def _src_layout(d_model):
    widths = [
        ("a_q", A_HEADS * HEAD_DIM), ("a_k", A_KV * HEAD_DIM), ("a_v", A_KV * HEAD_DIM),
        ("b_q", B_HEADS * HEAD_DIM),
        ("b_k_cmp", B_KV * HEAD_DIM), ("b_v_cmp", B_KV * HEAD_DIM),
        ("b_k_slc", B_KV * HEAD_DIM), ("b_v_slc", B_KV * HEAD_DIM),
        ("b_k_win", B_KV * HEAD_DIM), ("b_v_win", B_KV * HEAD_DIM),
        ("b_gate", 3 * B_HEADS),
        ("c_q", C_HEADS * HEAD_DIM), ("c_k", C_HEADS * HEAD_DIM), ("c_v", C_HEADS * HEAD_DIM),
        ("merge_gate", N_BRANCH * d_model),
    ]
    out, off = {}, 0
    for name, w in widths:
        out[name] = (off, off + w)
        off += w
    return out


def _cparams(sem):
    return pltpu.CompilerParams(dimension_semantics=sem, vmem_limit_bytes=VMEM_LIMIT)


def _resident(shape, index_map):
    return pl.BlockSpec(shape, index_map, pipeline_mode=pl.Buffered(1))


def _rms(x, gain):
    return x * lax.rsqrt(jnp.mean(x * x, axis=-1, keepdims=True) + EPS) * gain


def _modulated(x, mod_ref, sub, gain):
    shift = mod_ref[0, 3 * sub:3 * sub + 1, :]
    scale = mod_ref[0, 3 * sub + 1:3 * sub + 2, :]
    return _rms(x, gain) * (1.0 + scale) + shift


def _dot(a, b):
    return jnp.dot(a, b, preferred_element_type=F32)


def _dot_nt(a, b):
    return lax.dot_general(a, b, (((1,), (1,)), ((), ())), preferred_element_type=F32)


def _split3(x):
    hi = x.astype(BF16)
    r1 = x - hi.astype(F32)
    mid = r1.astype(BF16)
    lo = (r1 - mid.astype(F32)).astype(BF16)
    return hi, mid, lo


def _lane_half(shape):
    return (lax.broadcasted_iota(jnp.int32, shape, 1) % LANES) // HEAD_DIM


def _stack_group_queries(q, group, heads_per_group):
    half = _lane_half((q.shape[0], LANES))
    parts = []
    for hp in range(heads_per_group):
        h = group * heads_per_group + hp
        tile = q[:, (h // 2) * LANES:(h // 2 + 1) * LANES]
        if h % 2 != group:
            tile = pltpu.roll(tile, HEAD_DIM, 1)
        parts.append(jnp.where(half == group, tile, 0.0))
    scale = 1.0 / math.sqrt(HEAD_DIM)
    return (jnp.concatenate(parts, axis=0) * scale).astype(BF16)


def _unstack_group_outputs(pv, group, heads_per_group):
    tile = pv.shape[0] // heads_per_group
    half = _lane_half((tile, LANES))
    tiles = []
    for pair in range(heads_per_group // 2):
        even = pv[(2 * pair) * tile:(2 * pair + 1) * tile]
        odd = pv[(2 * pair + 1) * tile:(2 * pair + 2) * tile]
        if group != 0:
            even = pltpu.roll(even, HEAD_DIM, 1)
        if group != 1:
            odd = pltpu.roll(odd, HEAD_DIM, 1)
        tiles.append(jnp.where(half == 0, even, odd))
    return jnp.concatenate(tiles, axis=1)


def _mod_kernel(c_ref, w_ref, b_ref, o_ref):
    c = c_ref[...]
    act = (c * jax.nn.sigmoid(c)).astype(BF16)
    o_ref[0] = _dot(act, w_ref[0].astype(BF16)) + b_ref[0]


def _modulation(c, ada_w, ada_b):
    depth, d, n = ada_w.shape
    bsz = c.shape[0]
    tn = n // 4
    out = pl.pallas_call(
        _mod_kernel,
        grid=(depth, n // tn),
        in_specs=[
            pl.BlockSpec((bsz, d), lambda l, j: (0, 0)),
            pl.BlockSpec((1, d, tn), lambda l, j: (l, 0, j)),
            pl.BlockSpec((1, 1, tn), lambda l, j: (l, 0, j)),
        ],
        out_specs=pl.BlockSpec((1, bsz, tn), lambda l, j: (l, 0, j)),
        out_shape=jax.ShapeDtypeStruct((depth, bsz, n), F32),
        compiler_params=_cparams(("arbitrary", "arbitrary")),
        name="adaln_mod",
    )(c, ada_w, ada_b.reshape(depth, 1, n))
    return out.reshape(depth, bsz, N_SUB * 3, d)


def _ffn_kernel(x_ref, mod_ref, lnpre_ref, lnpost_ref, wg_ref, wu_ref, wd_ref, o_ref, acc_ref, *, sub, tf):
    x = x_ref[0]
    h = _modulated(x, mod_ref, sub, lnpre_ref[...]).astype(BF16)
    d_ff = wg_ref.shape[1]
    for j in range(d_ff // tf):
        g = _dot(h, wg_ref[:, j * tf:(j + 1) * tf])
        u = _dot(h, wu_ref[:, j * tf:(j + 1) * tf])
        a = (g * jax.nn.sigmoid(g) * u).astype(BF16)
        part = _dot(a, wd_ref[j * tf:(j + 1) * tf, :])
        if j == 0:
            acc_ref[...] = part
        else:
            acc_ref[...] += part
    gate = mod_ref[0, 3 * sub + 2:3 * sub + 3, :]
    o_ref[0] = x + FFN_RES * gate * _rms(acc_ref[...], lnpost_ref[...])


def _ffn(x, mod, ln_pre, ln_post, wg, wu, wd, sub, tm=512, tf=256):
    bsz, s, d = x.shape
    d_ff = wg.shape[1]
    row = lambda b, i: (b, i, 0)
    const2 = lambda b, i: (0, 0)
    return pl.pallas_call(
        functools.partial(_ffn_kernel, sub=sub, tf=tf),
        grid=(bsz, s // tm),
        in_specs=[
            pl.BlockSpec((1, tm, d), row),
            pl.BlockSpec((1, N_SUB * 3, d), lambda b, i: (b, 0, 0)),
            pl.BlockSpec((1, d), const2),
            pl.BlockSpec((1, d), const2),
            _resident((d, d_ff), const2),
            _resident((d, d_ff), const2),
            _resident((d_ff, d), const2),
        ],
        out_specs=pl.BlockSpec((1, tm, d), row),
        out_shape=jax.ShapeDtypeStruct(x.shape, F32),
        scratch_shapes=[pltpu.VMEM((tm, d), F32)],
        compiler_params=_cparams(("arbitrary", "arbitrary")),
        name="ffn",
    )(x, mod, ln_pre, ln_post, wg, wu, wd)


def _inproj_kernel(x_ref, mod_ref, lnpre_ref, watt_ref, wgate_ref, att_ref, gate_ref, *, att_chunk, gate_chunk):
    h = _modulated(x_ref[0], mod_ref, 1, lnpre_ref[...]).astype(BF16)
    for c0 in range(0, ATT_W, att_chunk):
        att_ref[0, :, c0:c0 + att_chunk] = _dot(h, watt_ref[:, c0:c0 + att_chunk]).astype(BF16)
    for c0 in range(0, GATE_W, gate_chunk):
        gate_ref[0, :, c0:c0 + gate_chunk] = _dot(h, wgate_ref[:, c0:c0 + gate_chunk]).astype(BF16)


def _inproj(x, mod, ln_pre, w_att, w_gate, tm=512):
    bsz, s, d = x.shape
    row = lambda b, i: (b, i, 0)
    const2 = lambda b, i: (0, 0)
    return pl.pallas_call(
        functools.partial(_inproj_kernel, att_chunk=512, gate_chunk=640),
        grid=(bsz, s // tm),
        in_specs=[
            pl.BlockSpec((1, tm, d), row),
            pl.BlockSpec((1, N_SUB * 3, d), lambda b, i: (b, 0, 0)),
            pl.BlockSpec((1, d), const2),
            _resident((d, ATT_W), const2),
            _resident((d, GATE_W), const2),
        ],
        out_specs=[pl.BlockSpec((1, tm, ATT_W), row), pl.BlockSpec((1, tm, GATE_W), row)],
        out_shape=[jax.ShapeDtypeStruct((bsz, s, ATT_W), BF16), jax.ShapeDtypeStruct((bsz, s, GATE_W), BF16)],
        compiler_params=_cparams(("arbitrary", "arbitrary")),
        name="inproj",
    )(x, mod, ln_pre, w_att, w_gate)


def _band_kernel(q_ref, k_ref, v_ref, bias_ref, o_ref, *, window, has_sink, q_tiles):
    nkb = window // QB + 1
    span = nkb * QB
    n_groups = bias_ref.shape[0]
    hpg = bias_ref.shape[1] // QB
    jk = lax.broadcasted_iota(jnp.int32, (1, span), 1)
    row0 = lax.broadcasted_iota(jnp.int32, (span, LANES), 0) == 0
    vhalf = _lane_half((span, LANES))
    chains = [(u, g) for u in range(q_tiles) for g in range(n_groups)]
    vspans, in_seqs, scores = [], [], []
    for u in range(q_tiles):
        i = pl.program_id(1) * q_tiles + u
        starts = [jnp.maximum(i - (nkb - 1) + kb, 0) * QB for kb in range(nkb)]
        kspan = jnp.concatenate([k_ref[0, pl.ds(pl.multiple_of(st, QB), QB), :] for st in starts], axis=0)
        vspan = jnp.concatenate([v_ref[0, pl.ds(pl.multiple_of(st, QB), QB), :] for st in starts], axis=0)
        in_seq = jk >= (nkb - 1 - i) * QB
        if has_sink:
            kspan = jnp.where(row0, jnp.zeros_like(kspan), kspan)
            vspan = jnp.where(row0, jnp.zeros_like(vspan), vspan)
            in_seq = in_seq | (jk == 0)
        vspans.append(vspan)
        in_seqs.append(in_seq)
        q = q_ref[0, u * QB:(u + 1) * QB, :].astype(F32)
        for g in range(n_groups):
            scores.append(_dot_nt(_stack_group_queries(q, g, hpg), kspan))
    weights = []
    for (u, g), z in zip(chains, scores):
        logits = jnp.where(in_seqs[u], z + bias_ref[g], NEG)
        weights.append(jnp.exp(logits - jnp.max(logits, axis=-1, keepdims=True)).astype(BF16))
    pvs = [_dot(e, jnp.where(vhalf == g, vspans[u], jnp.ones_like(vspans[u]))) for (u, g), e in zip(chains, weights)]
    for u in range(q_tiles):
        outs = []
        for g in range(n_groups):
            pv = pvs[u * n_groups + g]
            outs.append(_unstack_group_outputs(pv / pltpu.roll(pv, HEAD_DIM, 1), g, hpg))
        o_ref[0, u * QB:(u + 1) * QB, :] = jnp.concatenate(outs, axis=1).astype(BF16)


def _band_attention(att, q_col, k_col, v_col, bias, window, has_sink, q_tiles=4):
    bsz, s, _ = att.shape
    n_groups, rows, span = bias.shape
    width = n_groups * (rows // QB) * HEAD_DIM
    tq = q_tiles * QB
    return pl.pallas_call(
        functools.partial(_band_kernel, window=window, has_sink=has_sink, q_tiles=q_tiles),
        grid=(bsz, s // tq),
        in_specs=[
            pl.BlockSpec((1, tq, width), lambda b, i: (b, i, q_col // width)),
            pl.BlockSpec((1, s, LANES), lambda b, i: (b, 0, k_col // LANES)),
            pl.BlockSpec((1, s, LANES), lambda b, i: (b, 0, v_col // LANES)),
            _resident((n_groups, rows, span), lambda b, i: (0, 0, 0)),
        ],
        out_specs=pl.BlockSpec((1, tq, width), lambda b, i: (b, i, 0)),
        out_shape=jax.ShapeDtypeStruct((bsz, s, width), BF16),
        compiler_params=_cparams(("arbitrary", "arbitrary")),
        name=f"band{window}",
    )(att, att, att, bias)


def _compress_kernel(rk_ref, rv_ref, pos_ref, w1_ref, w2_ref, kc_ref, vc_ref):
    n_rows = rk_ref.shape[1]
    last = lax.broadcasted_iota(jnp.int32, (n_rows, 1), 0) == n_rows - 1
    for which, (r_ref, o_ref) in enumerate(((rk_ref, kc_ref), (rv_ref, vc_ref))):
        r = r_ref[0].astype(F32)
        lo = (r + pos_ref[which, 0]).astype(BF16)
        hi = (r + pos_ref[which, 1]).astype(BF16)
        halves = []
        for g in range(B_KV):
            hid = _dot(lo, w1_ref[which, 0, g]) + pltpu.roll(_dot(hi, w1_ref[which, 1, g]), n_rows - 1, 0)
            out = _dot(jax.nn.gelu(hid).astype(BF16), w2_ref[which])
            halves.append(jnp.where(last, 0.0, out))
        o_ref[0] = jnp.concatenate(halves, axis=1).astype(BF16)


def _compress(rk, rv, pos, w1, w2):
    bsz, n_rows, width = rk.shape
    g = B_KV
    r_spec = pl.BlockSpec((1, n_rows, width), lambda b: (b, 0, 0))
    o_spec = pl.BlockSpec((1, n_rows, g * HEAD_DIM), lambda b: (b, 0, 0))
    o_shape = jax.ShapeDtypeStruct((bsz, n_rows, g * HEAD_DIM), BF16)
    return pl.pallas_call(
        _compress_kernel,
        grid=(bsz,),
        in_specs=[
            r_spec, r_spec,
            pl.BlockSpec(pos.shape, lambda b: (0, 0, 0, 0)),
            pl.BlockSpec(w1.shape, lambda b: (0, 0, 0, 0, 0)),
            pl.BlockSpec(w2.shape, lambda b: (0, 0, 0)),
        ],
        out_specs=[o_spec, o_spec],
        out_shape=[o_shape, o_shape],
        compiler_params=_cparams(("arbitrary",)),
        name="nsa_compress",
    )(rk, rv, pos, w1, w2)


def _cmpsel_kernel(q_ref, kc_ref, vc_ref, ov_ref, o_ref, sel_ref, *, q_tiles):
    kc = kc_ref[0]
    vc = vc_ref[0]
    n_cmp_rows = kc.shape[0]
    hpg = B_HEADS // B_KV
    rows = hpg * QB
    row_q = lax.broadcasted_iota(jnp.int32, (rows, 1), 0) % QB
    n_idx = lax.broadcasted_iota(jnp.int32, (1, n_cmp_rows), 1)
    n_sel = ov_ref.shape[0]
    blk = lax.broadcasted_iota(jnp.int32, (n_sel, QB), 0)
    lane_q = lax.broadcasted_iota(jnp.int32, (n_sel, QB), 1)
    ov = ov_ref[...]
    chains = [(u, g) for u in range(q_tiles) for g in range(B_KV)]
    cmp_ok, forced, future, scores = [], [], [], []
    for u in range(q_tiles):
        t0 = (pl.program_id(1) * q_tiles + u) * QB
        cmp_ok.append((n_idx * CMP_STRIDE + CMP_BLOCK - 1 <= t0 + row_q) & (n_idx < n_cmp_rows - 1))
        cur = (t0 + lane_q) // SEL_BLOCK
        forced.append((blk == 0) | (blk == cur) | (blk == cur - 1))
        future.append(blk * SEL_BLOCK > t0 + lane_q)
        q = q_ref[0, u * QB:(u + 1) * QB, :].astype(F32)
        for g in range(B_KV):
            scores.append(_dot_nt(_stack_group_queries(q, g, hpg), kc))
    probs = []
    for (u, g), z in zip(chains, scores):
        logits = jnp.where(cmp_ok[u], z, NEG)
        m = jnp.max(logits, axis=-1, keepdims=True)
        e = jnp.where(cmp_ok[u], jnp.exp(logits - m), 0.0)
        ssum = jnp.sum(e, axis=-1, keepdims=True)
        probs.append(e / jnp.where(ssum > 0, ssum, 1.0))
    outs = [_unstack_group_outputs(_dot(p.astype(BF16), vc), g, hpg) for (u, g), p in zip(chains, probs)]
    imps = []
    for p in probs:
        p_heads = p[0:QB]
        for hp in range(1, hpg):
            p_heads = p_heads + p[hp * QB:(hp + 1) * QB]
        hi, mid, lo = _split3(p_heads)
        imps.append(_dot_nt(ov, hi) + _dot_nt(ov, mid) + _dot_nt(ov, lo))
    sel_t = []
    for (u, g), imp in zip(chains, imps):
        prio = jnp.where(forced[u], jnp.inf, jnp.where(future[u], -jnp.inf, imp))
        rank = jnp.zeros((n_sel, QB), F32)
        for c in range(n_sel):
            cand = prio[c:c + 1, :]
            rank = rank + jnp.where(blk > c, jnp.where(cand >= prio, 1.0, 0.0), jnp.where(cand > prio, 1.0, 0.0))
        sel_t.append(jnp.where(rank < min(SEL_TOPN, n_sel), 1.0, 0.0))
    pad = jnp.zeros((LANES - B_KV * n_sel, QB), F32)
    for u in range(q_tiles):
        o_ref[0, u * QB:(u + 1) * QB, :] = jnp.concatenate(outs[u * B_KV:(u + 1) * B_KV], axis=1).astype(BF16)
        sel_ref[0, u * QB:(u + 1) * QB, :] = jnp.concatenate(
            sel_t[u * B_KV:(u + 1) * B_KV] + [pad], axis=0).T.astype(BF16)


def _cmpsel(att, kc, vc, overlap, q_tiles=4):
    bsz, s, _ = att.shape
    n_rows = kc.shape[1]
    width = B_HEADS * HEAD_DIM
    tq = q_tiles * QB
    return pl.pallas_call(
        functools.partial(_cmpsel_kernel, q_tiles=q_tiles),
        grid=(bsz, s // tq),
        in_specs=[
            pl.BlockSpec((1, tq, width), lambda b, i: (b, i, ATT_COLS["b_q"] // width)),
            pl.BlockSpec((1, n_rows, LANES), lambda b, i: (b, 0, 0)),
            pl.BlockSpec((1, n_rows, LANES), lambda b, i: (b, 0, 0)),
            pl.BlockSpec(overlap.shape, lambda b, i: (0, 0)),
        ],
        out_specs=[pl.BlockSpec((1, tq, width), lambda b, i: (b, i, 0)),
                   pl.BlockSpec((1, tq, LANES), lambda b, i: (b, i, 0))],
        out_shape=[jax.ShapeDtypeStruct((bsz, s, width), BF16), jax.ShapeDtypeStruct((bsz, s, LANES), BF16)],
        compiler_params=_cparams(("arbitrary", "arbitrary")),
        name="nsa_cmpsel",
    )(att, kc, vc, overlap)


def _slc_kernel(q_ref, k_ref, v_ref, sel_ref, bias_ref, far_ref, o_ref, *, n_sel):
    i = pl.program_id(1)
    hpg = B_HEADS // B_KV
    tile = sel_ref.shape[1]
    rows = hpg * tile
    q = q_ref[0].astype(F32)
    sel = sel_ref[0]
    sel_row = lax.broadcasted_iota(jnp.int32, (LANES, tile), 0)
    key_blk = lax.broadcasted_iota(jnp.int32, (LANES, tile), 1) // SEL_BLOCK
    iq = lax.broadcasted_iota(jnp.int32, (tile, tile), 0)
    jk = lax.broadcasted_iota(jnp.int32, (tile, tile), 1)
    vhalf = _lane_half((tile, LANES))
    qs = [_stack_group_queries(q, g, hpg) for g in range(B_KV)]
    pair_rows = 2 * tile
    chains = [(g, r0) for g in range(B_KV) for r0 in range(0, rows, pair_rows)]

    def tile_step(j, carry, near, causal):
        st = pl.multiple_of(j * tile, tile)
        kt = k_ref[0, pl.ds(st, tile), :]
        vt = v_ref[0, pl.ds(st, tile), :]
        picked, zs, es, m_news, new = {}, {}, {}, {}, {}

        def stage(s, c):
            g, r0 = chains[c]
            if s == 0:
                if g not in picked:
                    expand = jnp.where(sel_row == g * n_sel + j * (tile // SEL_BLOCK) + key_blk, 1.0, 0.0)
                    ok = _dot(sel, expand.astype(BF16)) > 0.5
                    picked[g] = ok if causal is None else ok & causal
                zs[c] = _dot_nt(qs[g][r0:r0 + pair_rows], kt)
            elif s == 1:
                if near is None:
                    z = zs[c] + far_ref[g, r0:r0 + pair_rows]
                else:
                    z = zs[c] + bias_ref[g, r0:r0 + pair_rows, near * tile:(near + 1) * tile]
                logits = jnp.concatenate(
                    [jnp.where(picked[g], z[h * tile:(h + 1) * tile], NEG) for h in range(2)], axis=0)
                m_news[c] = jnp.maximum(carry[c][0], jnp.max(logits, axis=-1, keepdims=True))
                es[c] = jnp.exp(logits - m_news[c]).astype(BF16)
            else:
                m, acc = carry[c]
                pv = _dot(es[c], jnp.where(vhalf == g, vt, jnp.ones_like(vt)))
                new[c] = (m_news[c], jnp.exp(m - m_news[c]) * acc + pv)

        n_stages = 3
        for t in range(len(chains) + n_stages - 1):
            for s in range(n_stages):
                if 0 <= t - s < len(chains):
                    stage(s, t - s)
        return tuple(new[c] for c in range(len(chains)))

    init = tuple((jnp.full((pair_rows, 1), NEG, F32), jnp.zeros((pair_rows, LANES), F32)) for _ in chains)
    carry = tile_step(i, init, 1, jk <= iq)
    carry = tile_step(jnp.maximum(i - 1, 0), carry, 0, (i >= 1) & (jk >= 0))
    carry = lax.fori_loop(0, jnp.maximum(i - 1, 0), lambda j, c: tile_step(j, c, None, None), carry)
    outs = []
    for g in range(B_KV):
        acc = jnp.concatenate([carry[c][1] for c, (cg, _) in enumerate(chains) if cg == g], axis=0)
        outs.append(_unstack_group_outputs(acc / pltpu.roll(acc, HEAD_DIM, 1), g, hpg))
    o_ref[0] = jnp.concatenate(outs, axis=1).astype(BF16)


def _slc_attention(att, sel, bias_near, bias_far, n_sel):
    bsz, s, _ = att.shape
    width = B_HEADS * HEAD_DIM
    n_groups, rows, span = bias_near.shape
    tile = span // 2
    return pl.pallas_call(
        functools.partial(_slc_kernel, n_sel=n_sel),
        grid=(bsz, s // tile),
        in_specs=[
            pl.BlockSpec((1, tile, width), lambda b, i: (b, i, ATT_COLS["b_q"] // width)),
            pl.BlockSpec((1, s, LANES), lambda b, i: (b, 0, ATT_COLS["b_k_slc"] // LANES)),
            pl.BlockSpec((1, s, LANES), lambda b, i: (b, 0, ATT_COLS["b_v_slc"] // LANES)),
            pl.BlockSpec((1, tile, LANES), lambda b, i: (b, i, 0)),
            _resident((n_groups, rows, span), lambda b, i: (0, 0, 0)),
            pl.BlockSpec((n_groups, rows, 1), lambda b, i: (0, 0, 0)),
        ],
        out_specs=pl.BlockSpec((1, tile, width), lambda b, i: (b, i, 0)),
        out_shape=jax.ShapeDtypeStruct((bsz, s, width), BF16),
        compiler_params=_cparams(("arbitrary", "arbitrary")),
        name="nsa_slc",
    )(att, att, att, sel, bias_near, bias_far)


def _stick_kernel(q_ref, k_ref, v_ref, o_ref):
    i = pl.program_id(1)
    pairs = C_HEADS // 2
    tile = q_ref.shape[1]
    rows = 2 * tile
    half = _lane_half((tile, LANES))
    strict = (lax.broadcasted_iota(jnp.int32, (rows, tile), 1)
              < lax.broadcasted_iota(jnp.int32, (rows, tile), 0) % tile)
    later_than = jnp.where(lax.broadcasted_iota(jnp.int32, (2 * tile, tile), 0) % tile
                           > lax.broadcasted_iota(jnp.int32, (2 * tile, tile), 1), 1.0, 0.0).astype(BF16)
    qs = []
    for p in range(pairs):
        q = q_ref[0, :, p * LANES:(p + 1) * LANES].astype(F32) * (1.0 / math.sqrt(HEAD_DIM))
        qs.append(jnp.concatenate([jnp.where(half == 0, q, 0.0), jnp.where(half == 1, q, 0.0)],
                                  axis=0).astype(BF16))

    def tile_step(j, carry, mask):
        st = pl.multiple_of(j * tile, tile)
        zs, log_beta, log_keep, later, ws, new = {}, {}, {}, {}, {}, {}

        def stage(s, p):
            if s == 0:
                zs[p] = _dot_nt(qs[p], k_ref[0, pl.ds(st, tile), p * LANES:(p + 1) * LANES])
            elif s == 1:
                z = zs[p]
                neg_abs = pltpu.bitcast(pltpu.bitcast(z, jnp.uint32) | jnp.uint32(0x80000000), F32)
                lb = jnp.minimum(z, 0.0) - jnp.log(1.0 + jnp.exp(neg_abs))
                lk = lb - z
                if mask is not None:
                    lk = jnp.where(mask, lk, 0.0)
                log_beta[p], log_keep[p] = lb, lk
            elif s == 2:
                lk = log_keep[p]
                hi = lk.astype(BF16)
                later[p] = _dot(jnp.concatenate([hi, (lk - hi.astype(F32)).astype(BF16)], axis=1), later_than)
            elif s == 3:
                w = jnp.exp(log_beta[p] + later[p] + carry[p][0])
                if mask is not None:
                    w = jnp.where(mask, w, 0.0)
                ws[p] = w.astype(BF16)
            else:
                tail, acc = carry[p]
                vt = v_ref[0, pl.ds(st, tile), p * LANES:(p + 1) * LANES]
                new[p] = (tail + jnp.sum(log_keep[p], axis=-1, keepdims=True), acc + _dot(ws[p], vt))

        n_stages = 5
        for t in range(pairs + n_stages - 1):
            for s in range(n_stages):
                if 0 <= t - s < pairs:
                    stage(s, t - s)
        return tuple(new[p] for p in range(pairs))

    init = tuple((jnp.zeros((rows, 1), F32), jnp.zeros((rows, LANES), F32)) for _ in range(pairs))
    carry = tile_step(i, init, strict)
    carry = lax.fori_loop(0, i, lambda step, c: tile_step(i - 1 - step, c, None), carry)
    o_ref[0] = jnp.concatenate([jnp.where(half == 0, acc[0:tile], acc[tile:rows]) for _, acc in carry],
                               axis=1).astype(BF16)


def _stick_attention(att, tile=STICK_TILE):
    bsz, s, _ = att.shape
    width = C_HEADS * HEAD_DIM
    return pl.pallas_call(
        _stick_kernel,
        grid=(bsz, s // tile),
        in_specs=[
            pl.BlockSpec((1, tile, width), lambda b, i: (b, i, ATT_COLS["c_q"] // width)),
            pl.BlockSpec((1, s, width), lambda b, i: (b, 0, ATT_COLS["c_k"] // width)),
            pl.BlockSpec((1, s, width), lambda b, i: (b, 0, ATT_COLS["c_v"] // width)),
        ],
        out_specs=pl.BlockSpec((1, tile, width), lambda b, i: (b, i, 0)),
        out_shape=jax.ShapeDtypeStruct((bsz, s, width), BF16),
        compiler_params=_cparams(("arbitrary", "arbitrary")),
        name="stick",
    )(att, att, att)


def _merge_kernel(x_ref, mod_ref, lnpost_ref, ya_ref, oc_ref, os_ref, ow_ref, yc_ref, mg_ref, bg_ref,
                  wb_ref, wo_ref, o_ref):
    d = x_ref.shape[2]
    bw = ya_ref.shape[2]
    head_gate = jax.nn.sigmoid(bg_ref[0].astype(F32))
    gate_hi = head_gate.astype(BF16)
    gate_lo = (head_gate - gate_hi.astype(F32)).astype(BF16)
    gate_row = lax.broadcasted_iota(jnp.int32, (LANES, bw), 0)
    head_of_col = lax.broadcasted_iota(jnp.int32, (LANES, bw), 1) // HEAD_DIM
    y_b = jnp.zeros(ya_ref.shape[1:], F32)
    for br, ref in enumerate((oc_ref, os_ref, ow_ref)):
        expand = jnp.where(gate_row == br * B_HEADS + head_of_col, 1.0, 0.0).astype(BF16)
        y_b = y_b + (_dot(gate_hi, expand) + _dot(gate_lo, expand)) * ref[0].astype(F32)
    merged = jnp.zeros((x_ref.shape[1], d), F32)
    for br, y in enumerate((ya_ref[0], y_b.astype(BF16), yc_ref[0])):
        merged = merged + jax.nn.sigmoid(mg_ref[0, :, br * d:(br + 1) * d].astype(F32)) * _dot(y, wb_ref[br])
    t = _dot(merged.astype(BF16), wo_ref[...])
    gate = mod_ref[0, 5:6, :]
    o_ref[0] = x_ref[0] + gate * _rms(t, lnpost_ref[...])


def _merge(x, mod, ln_post, y_a, o_cmp, o_slc, o_win, y_c, gates, w_branch, w_out, tm=256):
    bsz, s, d = x.shape
    bw = y_a.shape[2]
    row = lambda b, i: (b, i, 0)
    const2 = lambda b, i: (0, 0)
    branch = pl.BlockSpec((1, tm, bw), row)
    return pl.pallas_call(
        _merge_kernel,
        grid=(bsz, s // tm),
        in_specs=[
            pl.BlockSpec((1, tm, d), row),
            pl.BlockSpec((1, N_SUB * 3, d), lambda b, i: (b, 0, 0)),
            pl.BlockSpec((1, d), const2),
            branch, branch, branch, branch, branch,
            pl.BlockSpec((1, tm, MERGE_W), row),
            pl.BlockSpec((1, tm, LANES), lambda b, i: (b, i, MERGE_W // LANES)),
            _resident(w_branch.shape, lambda b, i: (0, 0, 0)),
            _resident(w_out.shape, const2),
        ],
        out_specs=pl.BlockSpec((1, tm, d), row),
        out_shape=jax.ShapeDtypeStruct(x.shape, F32),
        compiler_params=_cparams(("arbitrary", "arbitrary")),
        name="merge",
    )(x, mod, ln_post, y_a, o_cmp, o_slc, o_win, y_c, gates, gates, w_branch, w_out)


def _t5_bucket(dist):
    max_exact = REL_BUCKETS // 2
    d = jnp.maximum(dist, 0)
    ratio = jnp.log(jnp.maximum(d, 1).astype(F32) / max_exact) / math.log(REL_MAX_DIST / max_exact)
    large = jnp.minimum(max_exact + (ratio * (REL_BUCKETS - max_exact)).astype(jnp.int32), REL_BUCKETS - 1)
    return jnp.where(d < max_exact, d, large)


def _band_bias(table, window, n_groups, mask_band=True, tile=QB):
    span = window + tile
    dist = window + jnp.arange(tile)[:, None] - jnp.arange(span)[None, :]
    onehot = (_t5_bucket(dist)[..., None] == jnp.arange(REL_BUCKETS)).astype(F32)
    bias = jnp.einsum('qkb,bh->hqk', onehot, table.astype(F32), precision=lax.Precision.HIGHEST)
    if mask_band:
        bias = jnp.where(((dist >= 0) & (dist < window))[None], bias, NEG)
    return bias.reshape(n_groups, -1, span)


def _with_sink_column(bias, sinks):
    n_groups, rows, _ = bias.shape
    col = jnp.repeat(sinks.astype(F32), QB).reshape(n_groups, rows)
    return bias.at[:, :, 0].set(col)


def _far_bucket_is_constant():
    d = np.arange(QB + 1, 1 << 16).astype(np.float32)
    max_exact = REL_BUCKETS // 2
    ratio = np.log(d / max_exact) / math.log(REL_MAX_DIST / max_exact)
    return bool(np.all(max_exact + (ratio * (REL_BUCKETS - max_exact)).astype(np.int32) >= REL_BUCKETS - 1))


def _overlap_matrix(n_rows, n_sel):
    cstart = np.arange(n_rows)[None, :] * CMP_STRIDE
    sstart = np.arange(n_sel)[:, None] * SEL_BLOCK
    ov = (cstart < sstart + SEL_BLOCK) & (cstart + CMP_BLOCK > sstart) & (np.arange(n_rows)[None, :] < n_rows - 1)
    return jnp.asarray(ov, dtype=BF16)


def _layer_weights(w_in_l, d):
    src = _src_layout(d)
    order = sorted(ATT_COLS, key=ATT_COLS.get)
    w_att = jnp.concatenate([w_in_l[:, src[n][0]:src[n][1]] for n in order], axis=1).astype(BF16)
    g0, g1 = src["b_gate"]
    m0, m1 = src["merge_gate"]
    pad = jnp.zeros((d, LANES - (g1 - g0)), w_in_l.dtype)
    w_gate = jnp.concatenate([w_in_l[:, m0:m1], w_in_l[:, g0:g1], pad], axis=1).astype(BF16)
    return w_att, w_gate


def _compress_rows(att, name):
    bsz, s, _ = att.shape
    c0 = ATT_COLS[name]
    return att[:, :, c0:c0 + LANES].reshape(bsz, s // CMP_STRIDE, CMP_STRIDE * LANES)


def _compress_params(pos, w1):
    hidden = w1.shape[-1]
    pos_rows = jnp.broadcast_to(pos.reshape(2, 2, CMP_STRIDE, 1, HEAD_DIM).astype(F32),
                                (2, 2, CMP_STRIDE, B_KV, HEAD_DIM)).reshape(2, 2, 1, CMP_STRIDE * LANES)
    w1r = w1.reshape(2, 2, CMP_STRIDE, HEAD_DIM, hidden)
    w1g = jnp.einsum('whjdc,gk->whgjkdc', w1r, jnp.eye(B_KV, dtype=w1.dtype))
    return pos_rows, w1g.reshape(2, 2, B_KV, CMP_STRIDE * LANES, hidden).astype(BF16)


def kernel(x, c, rel_bias, ada_w, ada_b, ln_pre, ln_post, ffn_w_gate, ffn_w_up, ffn_w_down, w_in, attn_sinks,
           cmp_pos, cmp_w1, cmp_w2, w_branch, w_out):
    bsz, s, d = x.shape
    depth = ada_w.shape[0]
    assert s % QB == 0 and s // CMP_STRIDE == LANES and CMP_BLOCK == 2 * CMP_STRIDE
    assert d == MERGE_W // N_BRANCH and A_WINDOW == QB and _far_bucket_is_constant()
    n_rows = s // CMP_STRIDE
    n_sel = s // SEL_BLOCK
    hpg_a, hpg_b = A_HEADS // A_KV, B_HEADS // B_KV

    mod = _modulation(c, ada_w, ada_b)
    table_a = rel_bias[:, :A_HEADS]
    table_b = rel_bias[:, A_HEADS:A_HEADS + B_HEADS]
    bias_a = _band_bias(table_a, A_WINDOW, A_KV)
    bias_w = _band_bias(table_b, B_WINDOW, B_KV)
    bias_near = _band_bias(table_b, SLC_TILE, B_KV, mask_band=False, tile=SLC_TILE)
    bias_far = jnp.repeat(table_b[REL_BUCKETS - 1].astype(F32), SLC_TILE).reshape(B_KV, hpg_b * SLC_TILE, 1)
    overlap = _overlap_matrix(n_rows, n_sel)

    for l in range(depth):
        mod_l = mod[l]
        x = _ffn(x, mod_l, ln_pre[l, 0:1], ln_post[l, 0:1], ffn_w_gate[l, 0].astype(BF16),
                 ffn_w_up[l, 0].astype(BF16), ffn_w_down[l, 0].astype(BF16), sub=0)

        w_att, w_gate = _layer_weights(w_in[l], d)
        att, gates = _inproj(x, mod_l, ln_pre[l, 1:2], w_att, w_gate)
        bias_a_l = _with_sink_column(bias_a, attn_sinks[l])
        y_a = _band_attention(att, ATT_COLS["a_q"], ATT_COLS["a_k"], ATT_COLS["a_v"], bias_a_l, A_WINDOW, True)
        pos_rows, w1_groups = _compress_params(cmp_pos[l], cmp_w1[l])
        kc, vc = _compress(_compress_rows(att, "b_k_cmp"), _compress_rows(att, "b_v_cmp"), pos_rows, w1_groups,
                           cmp_w2[l].astype(BF16))
        o_cmp, sel = _cmpsel(att, kc, vc, overlap)
        o_slc = _slc_attention(att, sel, bias_near, bias_far, n_sel)
        o_win = _band_attention(att, ATT_COLS["b_q"], ATT_COLS["b_k_win"], ATT_COLS["b_v_win"], bias_w, B_WINDOW,
                                False)
        y_c = _stick_attention(att)
        x = _merge(x, mod_l, ln_post[l, 1:2], y_a, o_cmp, o_slc, o_win, y_c, gates,
                   w_branch[l].astype(BF16), w_out[l].astype(BF16))

        x = _ffn(x, mod_l, ln_pre[l, 2:3], ln_post[l, 2:3], ffn_w_gate[l, 1].astype(BF16),
                 ffn_w_up[l, 1].astype(BF16), ffn_w_down[l, 1].astype(BF16), sub=2)
    return x
```
